```python
import math
import jax, jax.numpy as jnp
from jax import lax
import numpy as np

D_MODEL = 4096
BATCH = 4
SEQ = 4096
DEPTH = 1

CHUNK = 64
Q_BLOCK = 128
D_MIX = D_MODEL
D_ATTN = D_MIX // 2
D_LRU = D_MIX - D_ATTN
ATTN_HEADS = 16
ATTN_V_DIM = D_ATTN // ATTN_HEADS
ATTN_QK_DIM = ATTN_V_DIM // 2
LRU_BLOCKS = 16
LRU_BLOCK_DIM = D_LRU // LRU_BLOCKS
CONV_WIDTH = 4
LRU_C = 8.0
D_FF = 4 * D_MODEL
EPS = 1e-6
D_IN = 3 * D_ATTN + 2 * D_LRU

kernel_name = "hymba_diffattn_rglru_block"


def rmsnorm(x, g):
    x32 = x.astype(jnp.float32)
    y = x32 * lax.rsqrt(jnp.mean(x32 * x32, axis=-1, keepdims=True) + EPS)
    return (y * g.astype(jnp.float32)).astype(x.dtype)


def diff_attention(q, k, v, lam):
    B, S = q.shape[0], q.shape[1]
    nqb = S // Q_BLOCK
    scale = ATTN_QK_DIM ** -0.5
    q_blocks = q.reshape(B, nqb, Q_BLOCK, ATTN_HEADS, 2, ATTN_QK_DIM).transpose(1, 0, 2, 3, 4, 5)
    k_chunk = jnp.arange(S) // CHUNK

    def one_block(args):
        qb, idx = args
        q_chunk = (idx * Q_BLOCK + jnp.arange(Q_BLOCK)) // CHUNK
        visible = k_chunk[None, :] <= q_chunk[:, None]
        s = jnp.einsum('bqhmd,bkhmd->bhmqk', qb, k).astype(jnp.float32) * scale
        s = jnp.where(visible, s, -jnp.inf)
        p = jax.nn.softmax(s, axis=-1)
        a = p[:, :, 0] - lam * p[:, :, 1]
        return jnp.einsum('bhqk,bkhe->bqhe', a.astype(v.dtype), v)

    out = lax.map(one_block, (q_blocks, jnp.arange(nqb)))
    return out.transpose(1, 0, 2, 3, 4).reshape(B, S, ATTN_HEADS, ATTN_V_DIM)


def causal_depthwise_conv(x, w, b):
    S = x.shape[1]
    xp = jnp.pad(x, ((0, 0), (CONV_WIDTH - 1, 0), (0, 0)))
    y = b
    for j in range(CONV_WIDTH):
        y = y + w[j] * xp[:, j:j + S]
    return y


def rg_lru(x, w_r, b_r, w_i, b_i, lam_param):
    B, S = x.shape[0], x.shape[1]
    xb = x.reshape(B, S, LRU_BLOCKS, LRU_BLOCK_DIM)
    r = jax.nn.sigmoid(jnp.einsum('bsnd,nde->bsne', xb, w_r) + b_r).reshape(B, S, D_LRU)
    i = jax.nn.sigmoid(jnp.einsum('bsnd,nde->bsne', xb, w_i) + b_i).reshape(B, S, D_LRU)
    log_a = LRU_C * r.astype(jnp.float32) * jax.nn.log_sigmoid(lam_param.astype(jnp.float32))
    a = jnp.exp(log_a)
    b_term = jnp.sqrt(-jnp.expm1(2.0 * log_a)) * (i * x).astype(jnp.float32)

    def combine(left, right):
        a1, b1 = left
        a2, b2 = right
        return a1 * a2, a2 * b1 + b2

    _, h = lax.associative_scan(combine, (a, b_term), axis=1)
    return h.astype(x.dtype)


def setup_inputs(seed: int = 0) -> dict:
    key = jax.random.key(seed)
    ks = jax.random.split(key, 24)
    f32 = jnp.float32
    nrm = lambda k, shape, s: jax.random.normal(k, shape, f32) * s
    u = jax.random.uniform(ks[9], (DEPTH, D_LRU), f32, minval=0.9, maxval=0.999)
    a_base = u ** (1.0 / LRU_C)
    lru_lambda = jnp.log(a_base) - jnp.log1p(-a_base)
    return {
        "x": nrm(ks[0], (BATCH, SEQ, D_MODEL), 1.0),
        "norm_mix_g": 1.0 + nrm(ks[1], (DEPTH, D_MODEL), 0.02),
        "w_in": nrm(ks[2], (DEPTH, D_MODEL, D_IN), D_MODEL ** -0.5),
        "conv_w": nrm(ks[3], (DEPTH, CONV_WIDTH, D_LRU), CONV_WIDTH ** -0.5),
        "conv_b": nrm(ks[4], (DEPTH, D_LRU), 0.01),
        "w_rgate": nrm(ks[5], (DEPTH, LRU_BLOCKS, LRU_BLOCK_DIM, LRU_BLOCK_DIM), LRU_BLOCK_DIM ** -0.5),
        "b_rgate": nrm(ks[6], (DEPTH, LRU_BLOCKS, LRU_BLOCK_DIM), 0.01),
        "w_igate": nrm(ks[7], (DEPTH, LRU_BLOCKS, LRU_BLOCK_DIM, LRU_BLOCK_DIM), LRU_BLOCK_DIM ** -0.5),
        "b_igate": nrm(ks[8], (DEPTH, LRU_BLOCKS, LRU_BLOCK_DIM), 0.01),
        "lru_lambda": lru_lambda,
        "lambda_q1": nrm(ks[10], (DEPTH, ATTN_QK_DIM), 0.1),
        "lambda_k1": nrm(ks[11], (DEPTH, ATTN_QK_DIM), 0.1),
        "lambda_q2": nrm(ks[12], (DEPTH, ATTN_QK_DIM), 0.1),
        "lambda_k2": nrm(ks[13], (DEPTH, ATTN_QK_DIM), 0.1),
        "subln_g": 1.0 + nrm(ks[14], (DEPTH, ATTN_V_DIM), 0.02),
        "w_out": nrm(ks[15], (DEPTH, D_MIX, D_MODEL), D_MIX ** -0.5),
        "norm_mlp_g": 1.0 + nrm(ks[16], (DEPTH, D_MODEL), 0.02),
        "w_mlp_up": nrm(ks[17], (DEPTH, D_MODEL, D_FF), D_MODEL ** -0.5),
        "w_mlp_down": nrm(ks[18], (DEPTH, D_FF, D_MODEL), D_FF ** -0.5),
        "norm_final_g": 1.0 + nrm(ks[19], (D_MODEL,), 0.02),
    }


def reference(x, norm_mix_g, w_in, conv_w, conv_b, w_rgate, b_rgate, w_igate, b_igate,
              lru_lambda, lambda_q1, lambda_k1, lambda_q2, lambda_k2, subln_g, w_out,
              norm_mlp_g, w_mlp_up, w_mlp_down, norm_final_g):
    B, S = x.shape[0], x.shape[1]
    for l in range(DEPTH):
        lambda_init = 0.8 - 0.6 * math.exp(-0.3 * l)
        h = rmsnorm(x, norm_mix_g[l])
        proj = h @ w_in[l]
        q, k, v, x_rec, g_rec = jnp.split(
            proj, [D_ATTN, 2 * D_ATTN, 3 * D_ATTN, 3 * D_ATTN + D_LRU], axis=-1)

        q = q.reshape(B, S, ATTN_HEADS, 2, ATTN_QK_DIM)
        k = k.reshape(B, S, ATTN_HEADS, 2, ATTN_QK_DIM)
        v = v.reshape(B, S, ATTN_HEADS, ATTN_V_DIM)
        lam = (jnp.exp(jnp.sum(lambda_q1[l].astype(jnp.float32) * lambda_k1[l].astype(jnp.float32)))
               - jnp.exp(jnp.sum(lambda_q2[l].astype(jnp.float32) * lambda_k2[l].astype(jnp.float32)))
               + lambda_init)
        attn = diff_attention(q, k, v, lam)
        attn = (rmsnorm(attn, subln_g[l]) * (1.0 - lambda_init)).reshape(B, S, D_ATTN)

        xc = causal_depthwise_conv(x_rec, conv_w[l], conv_b[l])
        hr = rg_lru(xc, w_rgate[l], b_rgate[l], w_igate[l], b_igate[l], lru_lambda[l])
        rec = hr * jax.nn.gelu(g_rec, approximate=True)

        mixed = jnp.concatenate([attn, rec], axis=-1)
        x = x + mixed @ w_out[l]

        h2 = rmsnorm(x, norm_mlp_g[l])
        x = x + jnp.square(jax.nn.relu(h2 @ w_mlp_up[l])) @ w_mlp_down[l]
    return rmsnorm(x, norm_final_g)
```

```python
import functools
import math

import jax
import jax.numpy as jnp
from jax import lax
from jax.experimental import pallas as pl
from jax.experimental.pallas import tpu as pltpu

F32 = jnp.float32
BF16 = jnp.bfloat16

EPS = 1e-6
CHUNK = 64
ATTN_HEADS = 16
HEAD_V_DIM = 128
HEAD_QK_DIM = 64
LRU_BLOCKS = 16
LRU_BLOCK_DIM = 128
CONV_WIDTH = 4
LRU_C = 8.0

V7X_VMEM_LIMIT_BYTES = 56 * 1024 * 1024
SUBLANES = 8


def _params(*semantics):
    return pltpu.CompilerParams(dimension_semantics=semantics,
                                vmem_limit_bytes=V7X_VMEM_LIMIT_BYTES)


def _rmsnorm_rows(x, g):
    ms = jnp.mean(x * x, axis=-1, keepdims=True)
    return (x * lax.rsqrt(ms + EPS)) * g


def _norm_matmul_kernel(x_ref, g_ref, w_ref, o_ref, h_ref, *, relu2):
    @pl.when(pl.program_id(1) == 0)
    def _():
        h_ref[...] = _rmsnorm_rows(x_ref[...], g_ref[...]).astype(BF16)

    acc = jnp.dot(h_ref[...], w_ref[...], preferred_element_type=F32)
    if relu2:
        acc = jnp.square(jnp.maximum(acc, 0.0))
    o_ref[...] = acc.astype(o_ref.dtype)


def _norm_matmul(x, g, w, *, out_dtype, relu2, tm, tn):
    m, k = x.shape
    n = w.shape[1]
    return pl.pallas_call(
        functools.partial(_norm_matmul_kernel, relu2=relu2),
        grid=(m // tm, n // tn),
        in_specs=[
            pl.BlockSpec((tm, k), lambda i, j: (i, 0)),
            pl.BlockSpec((1, k), lambda i, j: (0, 0)),
            pl.BlockSpec((k, tn), lambda i, j: (0, j)),
        ],
        out_specs=pl.BlockSpec((tm, tn), lambda i, j: (i, j)),
        out_shape=jax.ShapeDtypeStruct((m, n), out_dtype),
        scratch_shapes=[pltpu.VMEM((tm, k), BF16)],
        compiler_params=_params("arbitrary", "arbitrary"),
        name="norm_matmul",
    )(x, g, w)


def _lambda_kernel(q1_ref, k1_ref, q2_ref, k2_ref, o_ref, *, lambda_init):
    d1 = jnp.sum(q1_ref[...] * k1_ref[...], axis=-1, keepdims=True)
    d2 = jnp.sum(q2_ref[...] * k2_ref[...], axis=-1, keepdims=True)
    o_ref[...] = jnp.exp(d1) - jnp.exp(d2) + lambda_init


def _diff_lambda(q1, k1, q2, k2, lambda_init):
    return pl.pallas_call(
        functools.partial(_lambda_kernel, lambda_init=lambda_init),
        out_shape=jax.ShapeDtypeStruct((1, 1), F32),
        name="diff_lambda",
    )(q1, k1, q2, k2)


def _attn_kernel(lam_ref, g_ref, q_ref, k_ref, v_ref, o_ref, acc1_ref, acc2_ref,
                 *, tq, out_scale):
    qi = pl.program_id(2)
    q = q_ref[0] * (HEAD_QK_DIM ** -0.5)
    lane = lax.broadcasted_iota(jnp.int32, q.shape, 1)
    zero = jnp.zeros_like(q)
    q1 = jnp.where(lane < HEAD_QK_DIM, q, zero)
    q2 = jnp.where(lane >= HEAD_QK_DIM, q, zero)
    nt = (((1,), (1,)), ((), ()))
    tn = (((0,), (0,)), ((), ()))

    def step(j, carry, masked):
        m1, l1, m2, l2 = carry
        rows = pl.ds(pl.multiple_of(j * tq, tq), tq)
        k = k_ref[0, rows, :]
        v = v_ref[0, rows, :]
        if masked:
            key_chunk = lax.broadcasted_iota(jnp.int32, (tq, tq), 0) // CHUNK
            qry_chunk = lax.broadcasted_iota(jnp.int32, (tq, tq), 1) // CHUNK
            visible = key_chunk <= qry_chunk
        out = []
        for qm, m, l, acc_ref in ((q1, m1, l1, acc1_ref), (q2, m2, l2, acc2_ref)):
            s = lax.dot_general(k, qm, nt, preferred_element_type=F32)
            if masked:
                s = jnp.where(visible, s, -jnp.inf)
            m_new = jnp.maximum(m, jnp.max(s, axis=0, keepdims=True))
            alpha = jnp.exp(m - m_new)
            p = jnp.exp(s - m_new)
            l_new = alpha * l + jnp.sum(p, axis=0, keepdims=True)
            pv = lax.dot_general(v, p.astype(BF16), tn, preferred_element_type=F32)
            acc_ref[...] = alpha * acc_ref[...] + pv
            out += [m_new, l_new]
        return tuple(out)

    acc1_ref[...] = jnp.zeros_like(acc1_ref)
    acc2_ref[...] = jnp.zeros_like(acc2_ref)
    neg = jnp.full((1, tq), -jnp.inf, F32)
    zer = jnp.zeros((1, tq), F32)
    carry = lax.fori_loop(0, qi, lambda j, c: step(j, c, False), (neg, zer, neg, zer))
    _, l1, _, l2 = step(qi, carry, True)

    a = acc1_ref[...] / l1 - lam_ref[0, 0] * (acc2_ref[...] / l2)
    ms = jnp.mean(a * a, axis=0, keepdims=True)
    y = ((a * lax.rsqrt(ms + EPS)) * g_ref[...]) * out_scale
    o_ref[0] = y.T.astype(o_ref.dtype)


def _diff_attention(qkv, lam, subln_g_col, *, batch, seq, tq, out_scale):
    h = ATTN_HEADS
    return pl.pallas_call(
        functools.partial(_attn_kernel, tq=tq, out_scale=out_scale),
        grid=(batch, h, seq // tq),
        in_specs=[
            pl.BlockSpec(memory_space=pltpu.SMEM),
            pl.BlockSpec((HEAD_V_DIM, 1), lambda b, hh, i: (0, 0)),
            pl.BlockSpec((1, tq, HEAD_V_DIM), lambda b, hh, i: (b, i, hh)),
            pl.BlockSpec((1, seq, HEAD_V_DIM), lambda b, hh, i: (b, 0, h + hh)),
            pl.BlockSpec((1, seq, HEAD_V_DIM), lambda b, hh, i: (b, 0, 2 * h + hh)),
        ],
        out_specs=pl.BlockSpec((1, tq, HEAD_V_DIM), lambda b, hh, i: (b, i, hh)),
        out_shape=jax.ShapeDtypeStruct((batch, seq, h * HEAD_V_DIM), BF16),
        scratch_shapes=[pltpu.VMEM((HEAD_V_DIM, tq), F32), pltpu.VMEM((HEAD_V_DIM, tq), F32)],
        compiler_params=_params("arbitrary", "arbitrary", "arbitrary"),
        name="diff_attention",
    )(lam, subln_g_col, qkv, qkv, qkv)


def _shift_rows(x, d, fill):
    pad = jnp.full((d,) + x.shape[1:], fill, x.dtype)
    return jnp.concatenate([pad, x[:-d]], axis=0)


def _lru_kernel(x_ref, gate_ref, cw_ref, cb_ref, wr_ref, br_ref, wi_ref, bi_ref, lam_ref,
                o_ref, tail_ref, h_ref, *, ts):
    t = pl.program_id(2)

    @pl.when(t == 0)
    def _():
        tail_ref[...] = jnp.zeros_like(tail_ref)
        h_ref[...] = jnp.zeros_like(h_ref)

    x = x_ref[0]
    xp = jnp.concatenate([tail_ref[...], x], axis=0)
    y = cb_ref[...]
    for j in range(CONV_WIDTH):
        off = SUBLANES - (CONV_WIDTH - 1) + j
        y = y + cw_ref[j:j + 1, :] * xp[off:off + ts]
    tail_ref[...] = x[ts - SUBLANES:]
    xc = y

    xb = xc.astype(BF16)
    r = jax.nn.sigmoid(jnp.dot(xb, wr_ref[0], preferred_element_type=F32) + br_ref[0])
    i = jax.nn.sigmoid(jnp.dot(xb, wi_ref[0], preferred_element_type=F32) + bi_ref[0])
    lam = lam_ref[...]
    log_sig = jnp.minimum(lam, 0.0) - jnp.log1p(jnp.exp(-jnp.abs(lam)))
    log_a = LRU_C * r * log_sig
    a = jnp.exp(log_a)
    b = jnp.sqrt(-jnp.tanh(log_a) * (1.0 + a * a)) * (i * xc)

    d = 1
    while d < ts:
        b = b + a * _shift_rows(b, d, 0.0)
        a = a * _shift_rows(a, d, 1.0)
        d *= 2
    h = b + a * h_ref[0:1, :]
    h_ref[...] = jnp.broadcast_to(h[ts - 1:ts, :], h_ref.shape)

    o_ref[0] = (h * jax.nn.gelu(gate_ref[0], approximate=True)).astype(o_ref.dtype)


def _rg_lru_branch(rg, conv_w, conv_b, w_r, b_r, w_i, b_i, lam, *, batch, seq, ts):
    nb, c = LRU_BLOCKS, LRU_BLOCK_DIM
    vec = lambda b, n, t: (0, n)
    blk = lambda b, n, t: (n, 0, 0)
    return pl.pallas_call(
        functools.partial(_lru_kernel, ts=ts),
        grid=(batch, nb, seq // ts),
        in_specs=[
            pl.BlockSpec((1, ts, c), lambda b, n, t: (b, t, n)),
            pl.BlockSpec((1, ts, c), lambda b, n, t: (b, t, nb + n)),
            pl.BlockSpec((CONV_WIDTH, c), vec),
            pl.BlockSpec((1, c), vec),
            pl.BlockSpec((1, c, c), blk),
            pl.BlockSpec((1, 1, c), blk),
            pl.BlockSpec((1, c, c), blk),
            pl.BlockSpec((1, 1, c), blk),
            pl.BlockSpec((1, c), vec),
        ],
        out_specs=pl.BlockSpec((1, ts, c), lambda b, n, t: (b, t, n)),
        out_shape=jax.ShapeDtypeStruct((batch, seq, nb * c), BF16),
        scratch_shapes=[pltpu.VMEM((SUBLANES, c), F32), pltpu.VMEM((SUBLANES, c), F32)],
        compiler_params=_params("arbitrary", "arbitrary", "arbitrary"),
        name="rg_lru",
    )(rg, rg, conv_w, conv_b, w_r, b_r, w_i, b_i, lam)


def _out_proj_kernel(a_ref, r_ref, wa_ref, wr_ref, x_ref, o_ref):
    acc = jnp.dot(a_ref[...], wa_ref[...], preferred_element_type=F32)
    acc = acc + jnp.dot(r_ref[...], wr_ref[...], preferred_element_type=F32)
    o_ref[...] = x_ref[...] + acc


def _out_proj(attn, rec, w_out, x, *, tm, tn):
    m, ka = attn.shape
    kr = rec.shape[1]
    n = w_out.shape[1]
    assert ka == kr
    return pl.pallas_call(
        _out_proj_kernel,
        grid=(m // tm, n // tn),
        in_specs=[
            pl.BlockSpec((tm, ka), lambda i, j: (i, 0)),
            pl.BlockSpec((tm, kr), lambda i, j: (i, 0)),
            pl.BlockSpec((ka, tn), lambda i, j: (0, j)),
            pl.BlockSpec((kr, tn), lambda i, j: (1, j)),
            pl.BlockSpec((tm, tn), lambda i, j: (i, j)),
        ],
        out_specs=pl.BlockSpec((tm, tn), lambda i, j: (i, j)),
        out_shape=jax.ShapeDtypeStruct((m, n), F32),
        compiler_params=_params("arbitrary", "arbitrary"),
        name="out_proj",
    )(attn, rec, w_out, w_out, x)


def _down_norm_kernel(u_ref, w_ref, x_ref, g_ref, o_ref, acc_ref):
    k = pl.program_id(1)

    @pl.when(k == 0)
    def _():
        acc_ref[...] = x_ref[...]

    acc_ref[...] += jnp.dot(u_ref[...], w_ref[...], preferred_element_type=F32)

    @pl.when(k == pl.num_programs(1) - 1)
    def _():
        o_ref[...] = _rmsnorm_rows(acc_ref[...], g_ref[...])


def _down_norm(u, w, x, g, *, tm, tk):
    m, kk = u.shape
    n = w.shape[1]
    return pl.pallas_call(
        _down_norm_kernel,
        grid=(m // tm, kk // tk),
        in_specs=[
            pl.BlockSpec((tm, tk), lambda i, k: (i, k)),
            pl.BlockSpec((tk, n), lambda i, k: (k, 0)),
            pl.BlockSpec((tm, n), lambda i, k: (i, 0), pipeline_mode=pl.Buffered(1)),
            pl.BlockSpec((1, n), lambda i, k: (0, 0)),
        ],
        out_specs=pl.BlockSpec((tm, n), lambda i, k: (i, 0)),
        out_shape=jax.ShapeDtypeStruct((m, n), F32),
        scratch_shapes=[pltpu.VMEM((tm, n), F32)],
        compiler_params=_params("arbitrary", "arbitrary"),
        name="down_norm",
    )(u, w, x, g)


def kernel(x, norm_mix_g, w_in, conv_w, conv_b, w_rgate, b_rgate, w_igate, b_igate, lru_lambda, lambda_q1, lambda_k1, lambda_q2, lambda_k2, subln_g, w_out, norm_mlp_g, w_mlp_up, w_mlp_down, norm_final_g):
    batch, seq, d_model = x.shape
    depth = w_in.shape[0]
    d_attn = ATTN_HEADS * HEAD_V_DIM
    d_lru = LRU_BLOCKS * LRU_BLOCK_DIM
    m = batch * seq
    row = lambda v: v.reshape(1, -1)

    xf = x.reshape(m, d_model)
    for l in range(depth):
        lambda_init = 0.8 - 0.6 * math.exp(-0.3 * l)
        w_in_l = w_in[l].astype(BF16)
        g_mix = row(norm_mix_g[l])
        qkv = _norm_matmul(xf, g_mix, w_in_l[:, :3 * d_attn], out_dtype=BF16, relu2=False,
                           tm=512, tn=1024)
        rg = _norm_matmul(xf, g_mix, w_in_l[:, 3 * d_attn:], out_dtype=F32, relu2=False,
                          tm=512, tn=1024)

        lam = _diff_lambda(row(lambda_q1[l]), row(lambda_k1[l]), row(lambda_q2[l]),
                           row(lambda_k2[l]), lambda_init)
        attn = _diff_attention(qkv.reshape(batch, seq, 3 * d_attn), lam,
                               subln_g[l].reshape(HEAD_V_DIM, 1),
                               batch=batch, seq=seq, tq=256, out_scale=1.0 - lambda_init)

        rec = _rg_lru_branch(
            rg.reshape(batch, seq, 2 * d_lru), conv_w[l], row(conv_b[l]),
            w_rgate[l].astype(BF16), b_rgate[l].reshape(LRU_BLOCKS, 1, LRU_BLOCK_DIM),
            w_igate[l].astype(BF16), b_igate[l].reshape(LRU_BLOCKS, 1, LRU_BLOCK_DIM),
            row(lru_lambda[l]), batch=batch, seq=seq, ts=256)

        xf = _out_proj(attn.reshape(m, d_attn), rec.reshape(m, d_lru), w_out[l].astype(BF16), xf,
                       tm=1024, tn=512)

        u = _norm_matmul(xf, row(norm_mlp_g[l]), w_mlp_up[l].astype(BF16), out_dtype=BF16,
                         relu2=True, tm=512, tn=1024)
        last = l == depth - 1
        assert last, "DEPTH > 1 needs an un-normalised down-projection variant"
        xf = _down_norm(u, w_mlp_down[l].astype(BF16), xf, row(norm_final_g), tm=512, tk=512)
    return xf.reshape(batch, seq, d_model)
```

```python
import functools
import math

import jax
import jax.numpy as jnp
from jax import lax
from jax.experimental import pallas as pl
from jax.experimental.pallas import tpu as pltpu

F32 = jnp.float32
BF16 = jnp.bfloat16

EPS = 1e-6
CHUNK = 64
ATTN_HEADS = 16
HEAD_V_DIM = 128
HEAD_QK_DIM = 64
LRU_BLOCKS = 16
LRU_BLOCK_DIM = 128
CONV_WIDTH = 4
LRU_C = 8.0

V7X_VMEM_LIMIT_BYTES = 56 * 1024 * 1024
SUBLANES = 8


def _params(*semantics):
    return pltpu.CompilerParams(dimension_semantics=semantics,
                                vmem_limit_bytes=V7X_VMEM_LIMIT_BYTES)


def _rmsnorm_rows(x, g):
    ms = jnp.mean(x * x, axis=-1, keepdims=True)
    return (x * lax.rsqrt(ms + EPS)) * g


def _norm_matmul_kernel(x_ref, g_ref, w_ref, o_ref, h_ref, *, relu2, scaled_tiles, scale):
    @pl.when(pl.program_id(1) == 0)
    def _():
        h_ref[...] = _rmsnorm_rows(x_ref[...], g_ref[...]).astype(BF16)

    acc = jnp.dot(h_ref[...], w_ref[...], preferred_element_type=F32)
    if relu2:
        acc = jnp.square(jnp.maximum(acc, 0.0))
    if scaled_tiles:
        acc = acc * jnp.where(pl.program_id(1) < scaled_tiles, scale, 1.0).astype(F32)
    o_ref[...] = acc.astype(o_ref.dtype)


def _norm_matmul(x, g, w, *, out_dtype, relu2, tm, tn, scaled_cols=0, scale=1.0):
    m, k = x.shape
    n = w.shape[1]
    assert scaled_cols % tn == 0
    return pl.pallas_call(
        functools.partial(_norm_matmul_kernel, relu2=relu2, scaled_tiles=scaled_cols // tn,
                          scale=scale),
        grid=(m // tm, n // tn),
        in_specs=[
            pl.BlockSpec((tm, k), lambda i, j: (i, 0)),
            pl.BlockSpec((1, k), lambda i, j: (0, 0)),
            pl.BlockSpec((k, tn), lambda i, j: (0, j)),
        ],
        out_specs=pl.BlockSpec((tm, tn), lambda i, j: (i, j)),
        out_shape=jax.ShapeDtypeStruct((m, n), out_dtype),
        scratch_shapes=[pltpu.VMEM((tm, k), BF16)],
        compiler_params=_params("arbitrary", "arbitrary"),
        name="norm_matmul",
    )(x, g, w)


def _lambda_kernel(q1_ref, k1_ref, q2_ref, k2_ref, o_ref, *, lambda_init):
    d1 = jnp.sum(q1_ref[...] * k1_ref[...], axis=-1, keepdims=True)
    d2 = jnp.sum(q2_ref[...] * k2_ref[...], axis=-1, keepdims=True)
    o_ref[...] = jnp.exp(d1) - jnp.exp(d2) + lambda_init


def _diff_lambda(q1, k1, q2, k2, lambda_init):
    return pl.pallas_call(
        functools.partial(_lambda_kernel, lambda_init=lambda_init),
        out_shape=jax.ShapeDtypeStruct((1, 1), F32),
        name="diff_lambda",
    )(q1, k1, q2, k2)


ATTN_ONES_ROWS = 16
ATTN_Q_PRESCALE = HEAD_QK_DIM ** -0.5 * math.log2(math.e)


def _attn_kernel(lam_ref, g_ref, q_ref, k_ref, v_ref, o_ref, vt_ref, acc1_ref, acc2_ref,
                 *, tq, n_tiles, out_scale):
    qi = pl.program_id(2)
    dv = HEAD_V_DIM

    @pl.when(qi == 0)
    def _():
        for j in range(n_tiles):
            vt = v_ref[0, j * tq:(j + 1) * tq, :].astype(F32).T
            vt_ref[j, 0:dv, :] = vt.astype(BF16)
            vt_ref[j, dv:dv + ATTN_ONES_ROWS, :] = jnp.ones((ATTN_ONES_ROWS, tq), BF16)

    q = q_ref[0]
    lane = lax.broadcasted_iota(jnp.int32, q.shape, 1)
    zero = jnp.zeros_like(q)
    q1 = jnp.where(lane < HEAD_QK_DIM, q, zero)
    q2 = jnp.where(lane >= HEAD_QK_DIM, q, zero)
    nt = (((1,), (1,)), ((), ()))

    def tiles(js, masked, carry):
        m = list(carry)
        scores = []
        for j in js:
            k = k_ref[0, pl.ds(pl.multiple_of(j * tq, tq), tq), :]
            scores.append([lax.dot_general(k, qm, nt, preferred_element_type=F32)
                           for qm in (q1, q2)])
        for t, j in enumerate(js):
            vt = vt_ref[j]
            if masked[t]:
                key_chunk = lax.broadcasted_iota(jnp.int32, (tq, tq), 0) // CHUNK
                qry_chunk = lax.broadcasted_iota(jnp.int32, (tq, tq), 1) // CHUNK
                visible = key_chunk <= qry_chunk
            for mi, acc_ref in enumerate((acc1_ref, acc2_ref)):
                s = scores[t][mi]
                if masked[t]:
                    s = jnp.where(visible, s, -jnp.inf)
                m_new = jnp.maximum(m[mi], jnp.max(s, axis=0, keepdims=True))
                alpha = jnp.exp2(m[mi] - m_new)
                p = jnp.exp2(s - m_new).astype(BF16)
                pv = jnp.dot(vt, p, preferred_element_type=F32)
                acc_ref[...] = alpha * acc_ref[...] + pv
                m[mi] = m_new
        return tuple(m)

    acc1_ref[...] = jnp.zeros_like(acc1_ref)
    acc2_ref[...] = jnp.zeros_like(acc2_ref)
    neg = jnp.full((1, tq), -jnp.inf, F32)
    carry = lax.fori_loop(0, lax.shift_right_logical(qi, 1),
                          lambda t, c: tiles([2 * t, 2 * t + 1], [False, False], c), (neg, neg))
    odd = lax.bitwise_and(qi, 1) == 1

    @pl.when(odd)
    def _():
        tiles([qi - 1, qi], [False, True], carry)

    @pl.when(jnp.logical_not(odd))
    def _():
        tiles([qi], [True], carry)

    o1 = acc1_ref[0:dv, :] / acc1_ref[dv:dv + 1, :]
    o2 = acc2_ref[0:dv, :] / acc2_ref[dv:dv + 1, :]
    a = o1 - lam_ref[0, 0] * o2
    ms = jnp.mean(a * a, axis=0, keepdims=True)
    y = ((a * lax.rsqrt(ms + EPS)) * g_ref[...]) * out_scale
    o_ref[0] = y.T.astype(o_ref.dtype)


def _diff_attention(qkv, lam, subln_g_col, *, batch, seq, out_scale, tq=512):
    h = ATTN_HEADS
    n_tiles = seq // tq
    acc_rows = HEAD_V_DIM + ATTN_ONES_ROWS
    return pl.pallas_call(
        functools.partial(_attn_kernel, tq=tq, n_tiles=n_tiles, out_scale=out_scale),
        grid=(batch, h, n_tiles),
        in_specs=[
            pl.BlockSpec(memory_space=pltpu.SMEM),
            pl.BlockSpec((HEAD_V_DIM, 1), lambda b, hh, i: (0, 0)),
            pl.BlockSpec((1, tq, HEAD_V_DIM), lambda b, hh, i: (b, i, hh)),
            pl.BlockSpec((1, seq, HEAD_V_DIM), lambda b, hh, i: (b, 0, h + hh)),
            pl.BlockSpec((1, seq, HEAD_V_DIM), lambda b, hh, i: (b, 0, 2 * h + hh)),
        ],
        out_specs=pl.BlockSpec((1, tq, HEAD_V_DIM), lambda b, hh, i: (b, i, hh)),
        out_shape=jax.ShapeDtypeStruct((batch, seq, h * HEAD_V_DIM), BF16),
        scratch_shapes=[pltpu.VMEM((n_tiles, acc_rows, tq), BF16),
                        pltpu.VMEM((acc_rows, tq), F32), pltpu.VMEM((acc_rows, tq), F32)],
        compiler_params=_params("arbitrary", "arbitrary", "arbitrary"),
        name="diff_attention",
    )(lam, subln_g_col, qkv, qkv, qkv)


def _lru_kernel(x_ref, gate_ref, cw_ref, cb_ref, wr_ref, br_ref, wi_ref, bi_ref, lam_ref,
                o_ref, tail_ref, h_ref, *, ts):
    t = pl.program_id(2)

    @pl.when(t == 0)
    def _():
        tail_ref[...] = jnp.zeros_like(tail_ref)
        h_ref[...] = jnp.zeros_like(h_ref)

    x = x_ref[0]
    xp = jnp.concatenate([tail_ref[...], x], axis=0)
    y = cb_ref[...]
    for j in range(CONV_WIDTH):
        off = SUBLANES - (CONV_WIDTH - 1) + j
        y = y + cw_ref[j:j + 1, :] * xp[off:off + ts]
    tail_ref[...] = x[ts - SUBLANES:]
    xc = y

    xb = xc.astype(BF16)
    r = jax.nn.sigmoid(jnp.dot(xb, wr_ref[0], preferred_element_type=F32) + br_ref[0])
    i = jax.nn.sigmoid(jnp.dot(xb, wi_ref[0], preferred_element_type=F32) + bi_ref[0])
    lam = lam_ref[...]
    log_sig = jnp.minimum(lam, 0.0) - jnp.log1p(jnp.exp(-jnp.abs(lam)))
    log_a = LRU_C * r * log_sig
    a = jnp.exp(log_a)
    b = jnp.sqrt(-jnp.tanh(log_a) * (1.0 + a * a)) * (i * xc)

    groups = ts // SUBLANES
    a = a.reshape(groups, SUBLANES, -1)
    b = b.reshape(groups, SUBLANES, -1)
    frame = lax.broadcasted_iota(jnp.int32, (1, SUBLANES, a.shape[-1]), 1)
    d = 1
    while d < SUBLANES:
        keep = frame >= d
        b = b + a * jnp.where(keep, pltpu.roll(b, d, axis=1), 0.0)
        a = a * jnp.where(keep, pltpu.roll(a, d, axis=1), 1.0)
        d *= 2
    carry = h_ref[0:1, :]
    hs = []
    for gi in range(groups):
        hg = b[gi] + a[gi] * carry
        carry = hg[SUBLANES - 1:SUBLANES, :]
        hs.append(hg)
    h = jnp.concatenate(hs, axis=0)
    h_ref[...] = jnp.broadcast_to(carry, h_ref.shape)

    o_ref[0] = (h * jax.nn.gelu(gate_ref[0], approximate=True)).astype(o_ref.dtype)


def _rg_lru_branch(rg, conv_w, conv_b, w_r, b_r, w_i, b_i, lam, *, batch, seq, ts=512):
    nb, c = LRU_BLOCKS, LRU_BLOCK_DIM
    vec = lambda b, n, t: (0, n)
    blk = lambda b, n, t: (n, 0, 0)
    return pl.pallas_call(
        functools.partial(_lru_kernel, ts=ts),
        grid=(batch, nb, seq // ts),
        in_specs=[
            pl.BlockSpec((1, ts, c), lambda b, n, t: (b, t, n)),
            pl.BlockSpec((1, ts, c), lambda b, n, t: (b, t, nb + n)),
            pl.BlockSpec((CONV_WIDTH, c), vec),
            pl.BlockSpec((1, c), vec),
            pl.BlockSpec((1, c, c), blk),
            pl.BlockSpec((1, 1, c), blk),
            pl.BlockSpec((1, c, c), blk),
            pl.BlockSpec((1, 1, c), blk),
            pl.BlockSpec((1, c), vec),
        ],
        out_specs=pl.BlockSpec((1, ts, c), lambda b, n, t: (b, t, n)),
        out_shape=jax.ShapeDtypeStruct((batch, seq, nb * c), BF16),
        scratch_shapes=[pltpu.VMEM((SUBLANES, c), F32), pltpu.VMEM((SUBLANES, c), F32)],
        compiler_params=_params("arbitrary", "arbitrary", "arbitrary"),
        name="rg_lru",
    )(rg, rg, conv_w, conv_b, w_r, b_r, w_i, b_i, lam)


def _out_proj_kernel(a_ref, r_ref, wa_ref, wr_ref, x_ref, o_ref):
    acc = jnp.dot(a_ref[...], wa_ref[...], preferred_element_type=F32) + x_ref[...]
    o_ref[...] = jnp.dot(r_ref[...], wr_ref[...], preferred_element_type=F32) + acc


def _out_proj(attn, rec, w_out, x, *, tm, tn):
    m, ka = attn.shape
    kr = rec.shape[1]
    n = w_out.shape[1]
    assert ka == kr
    return pl.pallas_call(
        _out_proj_kernel,
        grid=(m // tm, n // tn),
        in_specs=[
            pl.BlockSpec((tm, ka), lambda i, j: (i, 0)),
            pl.BlockSpec((tm, kr), lambda i, j: (i, 0)),
            pl.BlockSpec((ka, tn), lambda i, j: (0, j)),
            pl.BlockSpec((kr, tn), lambda i, j: (1, j)),
            pl.BlockSpec((tm, tn), lambda i, j: (i, j)),
        ],
        out_specs=pl.BlockSpec((tm, tn), lambda i, j: (i, j)),
        out_shape=jax.ShapeDtypeStruct((m, n), F32),
        compiler_params=_params("arbitrary", "arbitrary"),
        name="out_proj",
    )(attn, rec, w_out, w_out, x)


def _down_norm_kernel(u_ref, w_ref, x_ref, g_ref, o_ref):
    k = pl.program_id(1)

    @pl.when(k == 0)
    def _():
        o_ref[...] = x_ref[...]

    o_ref[...] = jnp.dot(u_ref[...], w_ref[...], preferred_element_type=F32) + o_ref[...]

    @pl.when(k == pl.num_programs(1) - 1)
    def _():
        o_ref[...] = _rmsnorm_rows(o_ref[...], g_ref[...])


def _down_norm(u, w, x, g, *, tm, tk):
    m, kk = u.shape
    n = w.shape[1]
    return pl.pallas_call(
        _down_norm_kernel,
        grid=(m // tm, kk // tk),
        in_specs=[
            pl.BlockSpec((tm, tk), lambda i, k: (i, k)),
            pl.BlockSpec((tk, n), lambda i, k: (k, 0)),
            pl.BlockSpec((tm, n), lambda i, k: (i, 0), pipeline_mode=pl.Buffered(1)),
            pl.BlockSpec((1, n), lambda i, k: (0, 0)),
        ],
        out_specs=pl.BlockSpec((tm, n), lambda i, k: (i, 0)),
        out_shape=jax.ShapeDtypeStruct((m, n), F32),
        compiler_params=_params("arbitrary", "arbitrary"),
        name="down_norm",
    )(u, w, x, g)


def kernel(x, norm_mix_g, w_in, conv_w, conv_b, w_rgate, b_rgate, w_igate, b_igate, lru_lambda, lambda_q1, lambda_k1, lambda_q2, lambda_k2, subln_g, w_out, norm_mlp_g, w_mlp_up, w_mlp_down, norm_final_g):
    batch, seq, d_model = x.shape
    depth = w_in.shape[0]
    d_attn = ATTN_HEADS * HEAD_V_DIM
    d_lru = LRU_BLOCKS * LRU_BLOCK_DIM
    m = batch * seq
    row = lambda v: v.reshape(1, -1)

    xf = x.reshape(m, d_model)
    for l in range(depth):
        lambda_init = 0.8 - 0.6 * math.exp(-0.3 * l)
        w_in_l = w_in[l].astype(BF16)
        g_mix = row(norm_mix_g[l])
        qkv = _norm_matmul(xf, g_mix, w_in_l[:, :3 * d_attn], out_dtype=BF16, relu2=False,
                           tm=512, tn=1024, scaled_cols=d_attn, scale=ATTN_Q_PRESCALE)
        rg = _norm_matmul(xf, g_mix, w_in_l[:, 3 * d_attn:], out_dtype=F32, relu2=False,
                          tm=512, tn=1024)

        lam = _diff_lambda(row(lambda_q1[l]), row(lambda_k1[l]), row(lambda_q2[l]),
                           row(lambda_k2[l]), lambda_init)
        attn = _diff_attention(qkv.reshape(batch, seq, 3 * d_attn), lam,
                               subln_g[l].reshape(HEAD_V_DIM, 1),
                               batch=batch, seq=seq, out_scale=1.0 - lambda_init)

        rec = _rg_lru_branch(
            rg.reshape(batch, seq, 2 * d_lru), conv_w[l], row(conv_b[l]),
            w_rgate[l].astype(BF16), b_rgate[l].reshape(LRU_BLOCKS, 1, LRU_BLOCK_DIM),
            w_igate[l].astype(BF16), b_igate[l].reshape(LRU_BLOCKS, 1, LRU_BLOCK_DIM),
            row(lru_lambda[l]), batch=batch, seq=seq)

        xf = _out_proj(attn.reshape(m, d_attn), rec.reshape(m, d_lru), w_out[l].astype(BF16), xf,
                       tm=1024, tn=512)

        u = _norm_matmul(xf, row(norm_mlp_g[l]), w_mlp_up[l].astype(BF16), out_dtype=BF16,
                         relu2=True, tm=512, tn=1024)
        last = l == depth - 1
        assert last, "DEPTH > 1 needs an un-normalised down-projection variant"
        xf = _down_norm(u, w_mlp_down[l].astype(BF16), xf, row(norm_final_g), tm=512, tk=1024)
    return xf.reshape(batch, seq, d_model)
```

```python
import functools
import math

import jax
import jax.numpy as jnp
from jax import lax
from jax.experimental import pallas as pl
from jax.experimental.pallas import tpu as pltpu

F32 = jnp.float32
BF16 = jnp.bfloat16

EPS = 1e-6
CHUNK = 64
ATTN_HEADS = 16
HEAD_V_DIM = 128
HEAD_QK_DIM = 64
LRU_BLOCKS = 16
LRU_BLOCK_DIM = 128
CONV_WIDTH = 4
LRU_C = 8.0

V7X_VMEM_LIMIT_BYTES = 56 * 1024 * 1024
SUBLANES = 8

IN_PROJ_TILES = dict(tm=512, tn=1024)
OUT_PROJ_TILES = dict(tm=1024, tn=512)
UP_PROJ_TILES = dict(tm=512, tn=1024)
DOWN_PROJ_TILES = dict(tm=512, tk=1024)
ATTN_TILE = 512
ATTN_HEADS_PER_STEP = 4
LRU_TILE = 512


def _params(*semantics):
    return pltpu.CompilerParams(dimension_semantics=semantics,
                                vmem_limit_bytes=V7X_VMEM_LIMIT_BYTES)


def _rmsnorm_rows(x, g):
    ms = jnp.mean(x * x, axis=-1, keepdims=True)
    return (x * lax.rsqrt(ms + EPS)) * g


def _norm_matmul_kernel(x_ref, g_ref, w_ref, o_ref, h_ref, *, relu2, scaled_tiles, scale):
    @pl.when(pl.program_id(1) == 0)
    def _():
        h_ref[...] = _rmsnorm_rows(x_ref[...], g_ref[...]).astype(BF16)

    acc = jnp.dot(h_ref[...], w_ref[...], preferred_element_type=F32)
    if relu2:
        acc = jnp.square(jnp.maximum(acc, 0.0))
    if scaled_tiles:
        acc = acc * jnp.where(pl.program_id(1) < scaled_tiles, scale, 1.0).astype(F32)
    o_ref[...] = acc.astype(o_ref.dtype)


def _norm_matmul(x, g, w, *, out_dtype, relu2, tm, tn, scaled_cols=0, scale=1.0):
    m, k = x.shape
    n = w.shape[1]
    assert scaled_cols % tn == 0
    return pl.pallas_call(
        functools.partial(_norm_matmul_kernel, relu2=relu2, scaled_tiles=scaled_cols // tn,
                          scale=scale),
        grid=(m // tm, n // tn),
        in_specs=[
            pl.BlockSpec((tm, k), lambda i, j: (i, 0)),
            pl.BlockSpec((1, k), lambda i, j: (0, 0)),
            pl.BlockSpec((k, tn), lambda i, j: (0, j)),
        ],
        out_specs=pl.BlockSpec((tm, tn), lambda i, j: (i, j)),
        out_shape=jax.ShapeDtypeStruct((m, n), out_dtype),
        scratch_shapes=[pltpu.VMEM((tm, k), BF16)],
        compiler_params=_params("arbitrary", "arbitrary"),
        name="norm_matmul",
    )(x, g, w)


def _lambda_kernel(q1_ref, k1_ref, q2_ref, k2_ref, o_ref, *, lambda_init):
    d1 = jnp.sum(q1_ref[...] * k1_ref[...], axis=-1, keepdims=True)
    d2 = jnp.sum(q2_ref[...] * k2_ref[...], axis=-1, keepdims=True)
    o_ref[...] = jnp.exp(d1) - jnp.exp(d2) + lambda_init


def _diff_lambda(q1, k1, q2, k2, lambda_init):
    return pl.pallas_call(
        functools.partial(_lambda_kernel, lambda_init=lambda_init),
        out_shape=jax.ShapeDtypeStruct((1, 1), F32),
        name="diff_lambda",
    )(q1, k1, q2, k2)


ATTN_ONES_ROWS = 16
ATTN_Q_PRESCALE = HEAD_QK_DIM ** -0.5 * math.log2(math.e)


def _attn_kernel(lam_ref, g_ref, q_ref, k_ref, v_ref, o_ref, vt_ref, qm_ref, acc_ref, m_ref,
                 sa_ref, sb_ref, *, tq, n_tiles, heads, out_scale):
    qi = pl.program_id(2)
    dv = HEAD_V_DIM
    head_lanes = lambda h: slice(h * dv, (h + 1) * dv)

    @pl.when(qi == 0)
    def _():
        for h in range(heads):
            for j in range(n_tiles):
                vt = v_ref[0, j * tq:(j + 1) * tq, head_lanes(h)].astype(F32).T
                vt_ref[h, j, 0:dv, :] = vt.astype(BF16)
                vt_ref[h, j, dv:dv + ATTN_ONES_ROWS, :] = jnp.ones((ATTN_ONES_ROWS, tq), BF16)

    lane = lax.broadcasted_iota(jnp.int32, (tq, dv), 1)
    for h in range(heads):
        q = q_ref[0, :, head_lanes(h)]
        zero = jnp.zeros_like(q)
        qm_ref[h, 0] = jnp.where(lane < HEAD_QK_DIM, q, zero)
        qm_ref[h, 1] = jnp.where(lane >= HEAD_QK_DIM, q, zero)
    acc_ref[...] = jnp.zeros_like(acc_ref)
    m_ref[...] = jnp.full(m_ref.shape, -jnp.inf, F32)
    nt = (((1,), (1,)), ((), ()))

    def scores_into(s_ref, j, h):
        k = k_ref[0, pl.ds(pl.multiple_of(j * tq, tq), tq), head_lanes(h)]
        for mi in range(2):
            s_ref[mi] = lax.dot_general(k, qm_ref[h, mi], nt, preferred_element_type=F32)

    def softmax_pv(s_ref, j, h, masked):
        vt = vt_ref[h, j]
        if masked:
            key_chunk = lax.broadcasted_iota(jnp.int32, (tq, tq), 0) // CHUNK
            qry_chunk = lax.broadcasted_iota(jnp.int32, (tq, tq), 1) // CHUNK
            visible = key_chunk <= qry_chunk
            read = lambda mi: jnp.where(visible, s_ref[mi], -jnp.inf)
        else:
            read = lambda mi: s_ref[mi]
        for mi in range(2):
            m_old = m_ref[h, mi, 0:1, :]
            m_new = jnp.maximum(m_old, jnp.max(read(mi), axis=0, keepdims=True))
            alpha = jnp.exp2(m_old - m_new)
            p = jnp.exp2(read(mi) - m_new).astype(BF16)
            pv = jnp.dot(vt, p, preferred_element_type=F32)
            acc_ref[h, mi] = alpha * acc_ref[h, mi] + pv
            m_ref[h, mi] = jnp.broadcast_to(m_new, m_ref.shape[2:])

    bufs = (sa_ref, sb_ref)
    scores_into(sa_ref, 0, 0)

    def one_tile(j, masked):
        for h in range(heads):
            if h + 1 < heads:
                scores_into(bufs[(h + 1) % 2], j, h + 1)
            elif not masked:
                scores_into(bufs[0], j + 1, 0)
            softmax_pv(bufs[h % 2], j, h, masked)

    def body(j, carry):
        one_tile(j, False)
        return carry

    lax.fori_loop(0, qi, body, 0)
    one_tile(qi, True)

    lam = lam_ref[0, 0]
    for h in range(heads):
        o1 = acc_ref[h, 0, 0:dv, :] / acc_ref[h, 0, dv:dv + 1, :]
        o2 = acc_ref[h, 1, 0:dv, :] / acc_ref[h, 1, dv:dv + 1, :]
        a = o1 - lam * o2
        ms = jnp.mean(a * a, axis=0, keepdims=True)
        y = ((a * lax.rsqrt(ms + EPS)) * g_ref[...]) * out_scale
        o_ref[0, :, head_lanes(h)] = y.T.astype(o_ref.dtype)


def _diff_attention(qkv, lam, subln_g_col, *, batch, seq, out_scale, tq=ATTN_TILE,
                    heads=ATTN_HEADS_PER_STEP):
    assert heads % 2 == 0 and ATTN_HEADS % heads == 0
    groups = ATTN_HEADS // heads
    n_tiles = seq // tq
    acc_rows = HEAD_V_DIM + ATTN_ONES_ROWS
    width = heads * HEAD_V_DIM
    return pl.pallas_call(
        functools.partial(_attn_kernel, tq=tq, n_tiles=n_tiles, heads=heads, out_scale=out_scale),
        grid=(batch, groups, n_tiles),
        in_specs=[
            pl.BlockSpec(memory_space=pltpu.SMEM),
            pl.BlockSpec((HEAD_V_DIM, 1), lambda b, g, i: (0, 0)),
            pl.BlockSpec((1, tq, width), lambda b, g, i: (b, i, g)),
            pl.BlockSpec((1, seq, width), lambda b, g, i: (b, 0, groups + g)),
            pl.BlockSpec((1, seq, width), lambda b, g, i: (b, 0, 2 * groups + g)),
        ],
        out_specs=pl.BlockSpec((1, tq, width), lambda b, g, i: (b, i, g)),
        out_shape=jax.ShapeDtypeStruct((batch, seq, ATTN_HEADS * HEAD_V_DIM), BF16),
        scratch_shapes=[pltpu.VMEM((heads, n_tiles, acc_rows, tq), BF16),
                        pltpu.VMEM((heads, 2, tq, HEAD_V_DIM), BF16),
                        pltpu.VMEM((heads, 2, acc_rows, tq), F32),
                        pltpu.VMEM((heads, 2, SUBLANES, tq), F32),
                        pltpu.VMEM((2, tq, tq), F32), pltpu.VMEM((2, tq, tq), F32)],
        compiler_params=_params("arbitrary", "arbitrary", "arbitrary"),
        name="diff_attention",
    )(lam, subln_g_col, qkv, qkv, qkv)


def _lru_kernel(x_ref, gate_ref, cw_ref, cb_ref, wr_ref, br_ref, wi_ref, bi_ref, lam_ref,
                o_ref, tail_ref, h_ref, *, ts):
    t = pl.program_id(2)

    @pl.when(t == 0)
    def _():
        tail_ref[...] = jnp.zeros_like(tail_ref)
        h_ref[...] = jnp.zeros_like(h_ref)

    x = x_ref[0]
    xp = jnp.concatenate([tail_ref[...], x], axis=0)
    y = cb_ref[...]
    for j in range(CONV_WIDTH):
        off = SUBLANES - (CONV_WIDTH - 1) + j
        y = y + cw_ref[j:j + 1, :] * xp[off:off + ts]
    tail_ref[...] = x[ts - SUBLANES:]
    xc = y

    xb = xc.astype(BF16)
    r = jax.nn.sigmoid(jnp.dot(xb, wr_ref[0], preferred_element_type=F32) + br_ref[0])
    i = jax.nn.sigmoid(jnp.dot(xb, wi_ref[0], preferred_element_type=F32) + bi_ref[0])
    lam = lam_ref[...]
    log_sig = jnp.minimum(lam, 0.0) - jnp.log1p(jnp.exp(-jnp.abs(lam)))
    log_a = LRU_C * r * log_sig
    a = jnp.exp(log_a)
    b = jnp.sqrt(-jnp.tanh(log_a) * (1.0 + a * a)) * (i * xc)

    groups = ts // SUBLANES
    a = a.reshape(groups, SUBLANES, -1)
    b = b.reshape(groups, SUBLANES, -1)
    frame = lax.broadcasted_iota(jnp.int32, (1, SUBLANES, a.shape[-1]), 1)
    d = 1
    while d < SUBLANES:
        keep = frame >= d
        b = b + a * jnp.where(keep, pltpu.roll(b, d, axis=1), 0.0)
        a = a * jnp.where(keep, pltpu.roll(a, d, axis=1), 1.0)
        d *= 2
    carry = h_ref[0:1, :]
    hs = []
    for gi in range(groups):
        hg = b[gi] + a[gi] * carry
        carry = hg[SUBLANES - 1:SUBLANES, :]
        hs.append(hg)
    h = jnp.concatenate(hs, axis=0)
    h_ref[...] = jnp.broadcast_to(carry, h_ref.shape)

    o_ref[0] = (h * jax.nn.gelu(gate_ref[0], approximate=True)).astype(o_ref.dtype)


def _rg_lru_branch(rg, conv_w, conv_b, w_r, b_r, w_i, b_i, lam, *, batch, seq, ts=LRU_TILE):
    nb, c = LRU_BLOCKS, LRU_BLOCK_DIM
    vec = lambda b, n, t: (0, n)
    blk = lambda b, n, t: (n, 0, 0)
    return pl.pallas_call(
        functools.partial(_lru_kernel, ts=ts),
        grid=(batch, nb, seq // ts),
        in_specs=[
            pl.BlockSpec((1, ts, c), lambda b, n, t: (b, t, n)),
            pl.BlockSpec((1, ts, c), lambda b, n, t: (b, t, nb + n)),
            pl.BlockSpec((CONV_WIDTH, c), vec),
            pl.BlockSpec((1, c), vec),
            pl.BlockSpec((1, c, c), blk),
            pl.BlockSpec((1, 1, c), blk),
            pl.BlockSpec((1, c, c), blk),
            pl.BlockSpec((1, 1, c), blk),
            pl.BlockSpec((1, c), vec),
        ],
        out_specs=pl.BlockSpec((1, ts, c), lambda b, n, t: (b, t, n)),
        out_shape=jax.ShapeDtypeStruct((batch, seq, nb * c), BF16),
        scratch_shapes=[pltpu.VMEM((SUBLANES, c), F32), pltpu.VMEM((SUBLANES, c), F32)],
        compiler_params=_params("arbitrary", "arbitrary", "arbitrary"),
        name="rg_lru",
    )(rg, rg, conv_w, conv_b, w_r, b_r, w_i, b_i, lam)


def _out_proj_kernel(a_ref, r_ref, wa_ref, wr_ref, x_ref, o_ref):
    acc = jnp.dot(a_ref[...], wa_ref[...], preferred_element_type=F32) + x_ref[...]
    o_ref[...] = jnp.dot(r_ref[...], wr_ref[...], preferred_element_type=F32) + acc


def _out_proj(attn, rec, w_out, x, *, tm, tn):
    m, ka = attn.shape
    kr = rec.shape[1]
    n = w_out.shape[1]
    assert ka == kr
    return pl.pallas_call(
        _out_proj_kernel,
        grid=(m // tm, n // tn),
        in_specs=[
            pl.BlockSpec((tm, ka), lambda i, j: (i, 0)),
            pl.BlockSpec((tm, kr), lambda i, j: (i, 0)),
            pl.BlockSpec((ka, tn), lambda i, j: (0, j)),
            pl.BlockSpec((kr, tn), lambda i, j: (1, j)),
            pl.BlockSpec((tm, tn), lambda i, j: (i, j)),
        ],
        out_specs=pl.BlockSpec((tm, tn), lambda i, j: (i, j)),
        out_shape=jax.ShapeDtypeStruct((m, n), F32),
        compiler_params=_params("arbitrary", "arbitrary"),
        name="out_proj",
    )(attn, rec, w_out, w_out, x)


def _down_norm_kernel(u_ref, w_ref, x_ref, g_ref, o_ref):
    k = pl.program_id(1)

    @pl.when(k == 0)
    def _():
        o_ref[...] = x_ref[...]

    o_ref[...] = jnp.dot(u_ref[...], w_ref[...], preferred_element_type=F32) + o_ref[...]

    @pl.when(k == pl.num_programs(1) - 1)
    def _():
        o_ref[...] = _rmsnorm_rows(o_ref[...], g_ref[...])


def _down_norm(u, w, x, g, *, tm, tk):
    m, kk = u.shape
    n = w.shape[1]
    return pl.pallas_call(
        _down_norm_kernel,
        grid=(m // tm, kk // tk),
        in_specs=[
            pl.BlockSpec((tm, tk), lambda i, k: (i, k)),
            pl.BlockSpec((tk, n), lambda i, k: (k, 0)),
            pl.BlockSpec((tm, n), lambda i, k: (i, 0), pipeline_mode=pl.Buffered(1)),
            pl.BlockSpec((1, n), lambda i, k: (0, 0)),
        ],
        out_specs=pl.BlockSpec((tm, n), lambda i, k: (i, 0)),
        out_shape=jax.ShapeDtypeStruct((m, n), F32),
        compiler_params=_params("arbitrary", "arbitrary"),
        name="down_norm",
    )(u, w, x, g)


def kernel(x, norm_mix_g, w_in, conv_w, conv_b, w_rgate, b_rgate, w_igate, b_igate, lru_lambda, lambda_q1, lambda_k1, lambda_q2, lambda_k2, subln_g, w_out, norm_mlp_g, w_mlp_up, w_mlp_down, norm_final_g):
    batch, seq, d_model = x.shape
    depth = w_in.shape[0]
    d_attn = ATTN_HEADS * HEAD_V_DIM
    d_lru = LRU_BLOCKS * LRU_BLOCK_DIM
    m = batch * seq
    row = lambda v: v.reshape(1, -1)

    xf = x.reshape(m, d_model)
    for l in range(depth):
        lambda_init = 0.8 - 0.6 * math.exp(-0.3 * l)
        w_in_l = w_in[l].astype(BF16)
        g_mix = row(norm_mix_g[l])
        qkv = _norm_matmul(xf, g_mix, w_in_l[:, :3 * d_attn], out_dtype=BF16, relu2=False,
                           scaled_cols=d_attn, scale=ATTN_Q_PRESCALE, **IN_PROJ_TILES)
        rg = _norm_matmul(xf, g_mix, w_in_l[:, 3 * d_attn:], out_dtype=F32, relu2=False,
                          **IN_PROJ_TILES)

        lam = _diff_lambda(row(lambda_q1[l]), row(lambda_k1[l]), row(lambda_q2[l]),
                           row(lambda_k2[l]), lambda_init)
        attn = _diff_attention(qkv.reshape(batch, seq, 3 * d_attn), lam,
                               subln_g[l].reshape(HEAD_V_DIM, 1),
                               batch=batch, seq=seq, out_scale=1.0 - lambda_init)

        rec = _rg_lru_branch(
            rg.reshape(batch, seq, 2 * d_lru), conv_w[l], row(conv_b[l]),
            w_rgate[l].astype(BF16), b_rgate[l].reshape(LRU_BLOCKS, 1, LRU_BLOCK_DIM),
            w_igate[l].astype(BF16), b_igate[l].reshape(LRU_BLOCKS, 1, LRU_BLOCK_DIM),
            row(lru_lambda[l]), batch=batch, seq=seq)

        xf = _out_proj(attn.reshape(m, d_attn), rec.reshape(m, d_lru), w_out[l].astype(BF16), xf,
                       **OUT_PROJ_TILES)

        u = _norm_matmul(xf, row(norm_mlp_g[l]), w_mlp_up[l].astype(BF16), out_dtype=BF16,
                         relu2=True, **UP_PROJ_TILES)
        last = l == depth - 1
        assert last, "DEPTH > 1 needs an un-normalised down-projection variant"
        xf = _down_norm(u, w_mlp_down[l].astype(BF16), xf, row(norm_final_g), **DOWN_PROJ_TILES)
    return xf.reshape(batch, seq, d_model)
```

```python
import functools
import math

import jax
import jax.numpy as jnp
from jax import lax
from jax.experimental import pallas as pl
from jax.experimental.pallas import tpu as pltpu

F32 = jnp.float32
BF16 = jnp.bfloat16

EPS = 1e-6
CHUNK = 64
ATTN_HEADS = 16
HEAD_V_DIM = 128
HEAD_QK_DIM = 64
LRU_BLOCKS = 16
LRU_BLOCK_DIM = 128
CONV_WIDTH = 4
LRU_C = 8.0

V7X_VMEM_LIMIT_BYTES = 56 * 1024 * 1024
SUBLANES = 8
BF16_ROW_TILE = 16

IN_PROJ_TILES = dict(tm=512, tn=1024)
OUT_PROJ_TILES = dict(tm=1024, tn=512)
UP_PROJ_TILES = dict(tm=512, tn=1024)
DOWN_PROJ_TILES = dict(tm=512, tk=1024)
ATTN_TILE = 512
ATTN_HEADS_PER_STEP = 4
LRU_TILE = 512
LRU_CHANNEL_TILE = 256


def _params(*semantics):
    return pltpu.CompilerParams(dimension_semantics=semantics,
                                vmem_limit_bytes=V7X_VMEM_LIMIT_BYTES)


def _rmsnorm_rows(x, g):
    ms = jnp.mean(x * x, axis=-1, keepdims=True)
    return (x * lax.rsqrt(ms + EPS)) * g


def _norm_matmul_kernel(x_ref, g_ref, w_ref, o_ref, h_ref, *, relu2, scaled_tiles, scale):
    @pl.when(pl.program_id(1) == 0)
    def _():
        h_ref[...] = _rmsnorm_rows(x_ref[...], g_ref[...]).astype(BF16)

    acc = jnp.dot(h_ref[...], w_ref[...], preferred_element_type=F32)
    if relu2:
        acc = jnp.square(jnp.maximum(acc, 0.0))
    if scaled_tiles:
        acc = acc * jnp.where(pl.program_id(1) < scaled_tiles, scale, 1.0).astype(F32)
    o_ref[...] = acc.astype(o_ref.dtype)


def _norm_matmul(x, g, w, *, out_dtype, relu2, tm, tn, scaled_cols=0, scale=1.0):
    m, k = x.shape
    n = w.shape[1]
    assert scaled_cols % tn == 0
    return pl.pallas_call(
        functools.partial(_norm_matmul_kernel, relu2=relu2, scaled_tiles=scaled_cols // tn,
                          scale=scale),
        grid=(m // tm, n // tn),
        in_specs=[
            pl.BlockSpec((tm, k), lambda i, j: (i, 0)),
            pl.BlockSpec((1, k), lambda i, j: (0, 0)),
            pl.BlockSpec((k, tn), lambda i, j: (0, j)),
        ],
        out_specs=pl.BlockSpec((tm, tn), lambda i, j: (i, j)),
        out_shape=jax.ShapeDtypeStruct((m, n), out_dtype),
        scratch_shapes=[pltpu.VMEM((tm, k), BF16)],
        compiler_params=_params("arbitrary", "arbitrary"),
        name="norm_matmul",
    )(x, g, w)


def _lambda_kernel(q1_ref, k1_ref, q2_ref, k2_ref, o_ref, *, lambda_init):
    d1 = jnp.sum(q1_ref[...] * k1_ref[...], axis=-1, keepdims=True)
    d2 = jnp.sum(q2_ref[...] * k2_ref[...], axis=-1, keepdims=True)
    o_ref[...] = jnp.exp(d1) - jnp.exp(d2) + lambda_init


def _diff_lambda(q1, k1, q2, k2, lambda_init):
    return pl.pallas_call(
        functools.partial(_lambda_kernel, lambda_init=lambda_init),
        out_shape=jax.ShapeDtypeStruct((1, 1), F32),
        name="diff_lambda",
    )(q1, k1, q2, k2)


ATTN_ONES_ROWS = 16
ATTN_Q_PRESCALE = HEAD_QK_DIM ** -0.5 * math.log2(math.e)


def _attn_kernel(lam_ref, g_ref, q_ref, k_ref, v_ref, *refs, tq, n_tiles, heads, out_scale,
                 n_cast):
    cast_in, o_ref, cast_out = refs[:n_cast], refs[n_cast], refs[n_cast + 1:2 * n_cast + 1]
    vt_ref, qm_ref, acc_ref, m_ref, sa_ref, sb_ref = refs[2 * n_cast + 1:]
    qi = pl.program_id(2)
    dv = HEAD_V_DIM
    head_lanes = lambda h: slice(h * dv, (h + 1) * dv)

    for w_ref, wb_ref in zip(cast_in, cast_out):
        wb_ref[...] = w_ref[...].astype(BF16)

    @pl.when(qi == 0)
    def _():
        for h in range(heads):
            for j in range(n_tiles):
                vt = v_ref[0, j * tq:(j + 1) * tq, head_lanes(h)].astype(F32).T
                vt_ref[h, j, 0:dv, :] = vt.astype(BF16)
                vt_ref[h, j, dv:dv + ATTN_ONES_ROWS, :] = jnp.ones((ATTN_ONES_ROWS, tq), BF16)

    lane = lax.broadcasted_iota(jnp.int32, (tq, dv), 1)
    for h in range(heads):
        q = q_ref[0, :, head_lanes(h)]
        zero = jnp.zeros_like(q)
        qm_ref[h, 0] = jnp.where(lane < HEAD_QK_DIM, q, zero)
        qm_ref[h, 1] = jnp.where(lane >= HEAD_QK_DIM, q, zero)
    acc_ref[...] = jnp.zeros_like(acc_ref)
    m_ref[...] = jnp.full(m_ref.shape, -jnp.inf, F32)
    nt = (((1,), (1,)), ((), ()))

    def scores_into(s_ref, j, h):
        k = k_ref[0, pl.ds(pl.multiple_of(j * tq, tq), tq), head_lanes(h)]
        for mi in range(2):
            s_ref[mi] = lax.dot_general(k, qm_ref[h, mi], nt, preferred_element_type=F32)

    def softmax_pv(s_ref, j, h, masked):
        vt = vt_ref[h, j]
        if masked:
            key_chunk = lax.broadcasted_iota(jnp.int32, (tq, tq), 0) // CHUNK
            qry_chunk = lax.broadcasted_iota(jnp.int32, (tq, tq), 1) // CHUNK
            visible = key_chunk <= qry_chunk
            read = lambda mi: jnp.where(visible, s_ref[mi], -jnp.inf)
        else:
            read = lambda mi: s_ref[mi]
        for mi in range(2):
            m_old = m_ref[h, mi, 0:1, :]
            m_new = jnp.maximum(m_old, jnp.max(read(mi), axis=0, keepdims=True))
            alpha = jnp.exp2(m_old - m_new)
            p = jnp.exp2(read(mi) - m_new).astype(BF16)
            pv = jnp.dot(vt, p, preferred_element_type=F32)
            acc_ref[h, mi] = alpha * acc_ref[h, mi] + pv
            m_ref[h, mi] = jnp.broadcast_to(m_new, m_ref.shape[2:])

    bufs = (sa_ref, sb_ref)
    scores_into(sa_ref, 0, 0)

    def one_tile(j, masked):
        for h in range(heads):
            if h + 1 < heads:
                scores_into(bufs[(h + 1) % 2], j, h + 1)
            elif not masked:
                scores_into(bufs[0], j + 1, 0)
            softmax_pv(bufs[h % 2], j, h, masked)

    def body(j, carry):
        one_tile(j, False)
        return carry

    lax.fori_loop(0, qi, body, 0)
    one_tile(qi, True)

    lam = lam_ref[0, 0]
    for h in range(heads):
        o1 = acc_ref[h, 0, 0:dv, :] / acc_ref[h, 0, dv:dv + 1, :]
        o2 = acc_ref[h, 1, 0:dv, :] / acc_ref[h, 1, dv:dv + 1, :]
        a = o1 - lam * o2
        ms = jnp.mean(a * a, axis=0, keepdims=True)
        y = ((a * lax.rsqrt(ms + EPS)) * g_ref[...]) * out_scale
        o_ref[0, :, head_lanes(h)] = y.T.astype(o_ref.dtype)


def _diff_attention(qkv, lam, subln_g_col, cast_weights=(), *, batch, seq, out_scale,
                    tq=ATTN_TILE, heads=ATTN_HEADS_PER_STEP):
    assert heads % 2 == 0 and ATTN_HEADS % heads == 0
    groups = ATTN_HEADS // heads
    n_tiles = seq // tq
    acc_rows = HEAD_V_DIM + ATTN_ONES_ROWS
    width = heads * HEAD_V_DIM
    n_steps = batch * groups * n_tiles
    step = lambda b, g, i: (b * groups + g) * n_tiles + i
    cast_specs = []
    for w in cast_weights:
        assert w.shape[0] % (n_steps * BF16_ROW_TILE) == 0
        cast_specs.append(pl.BlockSpec((w.shape[0] // n_steps, w.shape[1]),
                                       lambda b, g, i: (step(b, g, i), 0)))
    return pl.pallas_call(
        functools.partial(_attn_kernel, tq=tq, n_tiles=n_tiles, heads=heads, out_scale=out_scale,
                          n_cast=len(cast_weights)),
        grid=(batch, groups, n_tiles),
        in_specs=[
            pl.BlockSpec(memory_space=pltpu.SMEM),
            pl.BlockSpec((HEAD_V_DIM, 1), lambda b, g, i: (0, 0)),
            pl.BlockSpec((1, tq, width), lambda b, g, i: (b, i, g)),
            pl.BlockSpec((1, seq, width), lambda b, g, i: (b, 0, groups + g)),
            pl.BlockSpec((1, seq, width), lambda b, g, i: (b, 0, 2 * groups + g)),
        ] + cast_specs,
        out_specs=[pl.BlockSpec((1, tq, width), lambda b, g, i: (b, i, g))] + cast_specs,
        out_shape=[jax.ShapeDtypeStruct((batch, seq, ATTN_HEADS * HEAD_V_DIM), BF16)]
        + [jax.ShapeDtypeStruct(w.shape, BF16) for w in cast_weights],
        scratch_shapes=[pltpu.VMEM((heads, n_tiles, acc_rows, tq), BF16),
                        pltpu.VMEM((heads, 2, tq, HEAD_V_DIM), BF16),
                        pltpu.VMEM((heads, 2, acc_rows, tq), F32),
                        pltpu.VMEM((heads, 2, SUBLANES, tq), F32),
                        pltpu.VMEM((2, tq, tq), F32), pltpu.VMEM((2, tq, tq), F32)],
        compiler_params=_params("arbitrary", "arbitrary", "arbitrary"),
        name="diff_attention",
    )(lam, subln_g_col, qkv, qkv, qkv, *cast_weights)


def _causal_conv(x, tail, cw, cb):
    ts = x.shape[0]
    xp = jnp.concatenate([tail, x], axis=0)
    xc = cb
    for j in range(CONV_WIDTH):
        off = SUBLANES - (CONV_WIDTH - 1) + j
        xc = xc + cw[j:j + 1, :] * xp[off:off + ts]
    return xc


def _block_gate(xb, w_ref, b_ref):
    blocks = [jnp.dot(xb[:, n * LRU_BLOCK_DIM:(n + 1) * LRU_BLOCK_DIM], w_ref[n],
                      preferred_element_type=F32) + b_ref[n]
              for n in range(xb.shape[1] // LRU_BLOCK_DIM)]
    return jax.nn.sigmoid(jnp.concatenate(blocks, axis=1))


def _lru_scan(xc, r, i, gate_act, h0, lam):
    ts, c = xc.shape
    log_sig = jnp.minimum(lam, 0.0) - jnp.log1p(jnp.exp(-jnp.abs(lam)))
    log_a = LRU_C * r * log_sig
    a = jnp.exp(log_a)
    b = jnp.sqrt(-jnp.tanh(log_a) * (1.0 + a * a)) * (i * xc)

    groups = ts // SUBLANES
    a = a.reshape(groups, SUBLANES, c)
    b = b.reshape(groups, SUBLANES, c)
    frame = lax.broadcasted_iota(jnp.int32, (1, SUBLANES, c), 1)
    d = 1
    while d < SUBLANES:
        keep = frame >= d
        b = b + a * jnp.where(keep, pltpu.roll(b, d, axis=1), 0.0)
        a = a * jnp.where(keep, pltpu.roll(a, d, axis=1), 1.0)
        d *= 2
    carry = h0
    hs = []
    for gi in range(groups):
        hg = b[gi] + a[gi] * carry
        carry = hg[SUBLANES - 1:SUBLANES, :]
        hs.append(hg)
    return jnp.concatenate(hs, axis=0) * gate_act, carry


def _rec_branch_kernel(x_ref, g_ref, wx_ref, wg_ref, cw_ref, cb_ref, wr_ref, br_ref, wi_ref,
                       bi_ref, lam_ref, o_ref, h_ref, xr_ref, gr_ref, tail_ref, state_ref,
                       *, nj, tiles_per_seq, n_tiles):
    s = pl.program_id(0)
    j = lax.rem(jnp.minimum(s, n_tiles - 1), nj)
    prev = jnp.maximum(s - 1, 0)
    ip = prev // nj
    jp = lax.rem(prev, nj)
    ts = xr_ref.shape[0]

    @pl.when(s == 0)
    def _():
        xr_ref[...] = jnp.zeros_like(xr_ref)
        gr_ref[...] = jnp.zeros_like(gr_ref)
        tail_ref[...] = jnp.zeros_like(tail_ref)
        state_ref[...] = jnp.zeros_like(state_ref)

    @pl.when(j == 0)
    def _():
        h_ref[...] = _rmsnorm_rows(x_ref[...], g_ref[...]).astype(BF16)

    seq_start = lax.rem(ip, tiles_per_seq) == 0
    tail = jnp.where(seq_start, 0.0, tail_ref[jp])
    h0 = jnp.where(seq_start, 0.0, state_ref[jp, 0:1, :])
    x = xr_ref[...]
    gate_act = jax.nn.gelu(gr_ref[...], approximate=True)
    xc = _causal_conv(x, tail, cw_ref[...], cb_ref[...])
    xr_ref[...] = jnp.dot(h_ref[...], wx_ref[...], preferred_element_type=F32)
    xb = xc.astype(BF16)
    r = _block_gate(xb, wr_ref, br_ref)
    i = _block_gate(xb, wi_ref, bi_ref)
    gr_ref[...] = jnp.dot(h_ref[...], wg_ref[...], preferred_element_type=F32)
    rec, h_last = _lru_scan(xc, r, i, gate_act, h0, lam_ref[...])
    o_ref[...] = rec.astype(o_ref.dtype)
    tail_ref[jp] = x[ts - SUBLANES:]
    state_ref[jp] = jnp.broadcast_to(h_last, state_ref.shape[1:])


def _rec_branch(x, g, w_in, col_x, col_g, conv_w, conv_b, w_r, b_r, w_i, b_i, lam, *, seq,
                tm=LRU_TILE, tc=LRU_CHANNEL_TILE):
    m, k = x.shape
    d_lru = LRU_BLOCKS * LRU_BLOCK_DIM
    assert seq % tm == 0 and d_lru % tc == 0 and col_x % tc == 0 and col_g % tc == 0
    nj = d_lru // tc
    n_tiles = (m // tm) * nj
    nblk = tc // LRU_BLOCK_DIM
    this_i = lambda s: jnp.minimum(s, n_tiles - 1) // nj
    this_j = lambda s: lax.rem(jnp.minimum(s, n_tiles - 1), nj)
    prev_i = lambda s: jnp.maximum(s - 1, 0) // nj
    prev_j = lambda s: lax.rem(jnp.maximum(s - 1, 0), nj)
    vec = lambda s: (0, prev_j(s))
    blk = lambda s: (prev_j(s), 0, 0)
    return pl.pallas_call(
        functools.partial(_rec_branch_kernel, nj=nj, tiles_per_seq=seq // tm, n_tiles=n_tiles),
        grid=(n_tiles + 1,),
        in_specs=[
            pl.BlockSpec((tm, k), lambda s: (this_i(s), 0)),
            pl.BlockSpec((1, k), lambda s: (0, 0)),
            pl.BlockSpec((k, tc), lambda s: (0, col_x // tc + this_j(s))),
            pl.BlockSpec((k, tc), lambda s: (0, col_g // tc + this_j(s))),
            pl.BlockSpec((CONV_WIDTH, tc), vec),
            pl.BlockSpec((1, tc), vec),
            pl.BlockSpec((nblk, LRU_BLOCK_DIM, LRU_BLOCK_DIM), blk),
            pl.BlockSpec((nblk, 1, LRU_BLOCK_DIM), blk),
            pl.BlockSpec((nblk, LRU_BLOCK_DIM, LRU_BLOCK_DIM), blk),
            pl.BlockSpec((nblk, 1, LRU_BLOCK_DIM), blk),
            pl.BlockSpec((1, tc), vec),
        ],
        out_specs=pl.BlockSpec((tm, tc), lambda s: (prev_i(s), prev_j(s))),
        out_shape=jax.ShapeDtypeStruct((m, d_lru), BF16),
        scratch_shapes=[pltpu.VMEM((tm, k), BF16),
                        pltpu.VMEM((tm, tc), F32), pltpu.VMEM((tm, tc), F32),
                        pltpu.VMEM((nj, SUBLANES, tc), F32),
                        pltpu.VMEM((nj, SUBLANES, tc), F32)],
        compiler_params=_params("arbitrary"),
        name="rec_branch",
    )(x, g, w_in, w_in, conv_w, conv_b, w_r, b_r, w_i, b_i, lam)


def _out_proj_kernel(a_ref, r_ref, wa_ref, wr_ref, x_ref, o_ref):
    acc = jnp.dot(a_ref[...], wa_ref[...], preferred_element_type=F32) + x_ref[...]
    o_ref[...] = jnp.dot(r_ref[...], wr_ref[...], preferred_element_type=F32) + acc


def _out_proj(attn, rec, w_out, x, *, tm, tn):
    m, ka = attn.shape
    kr = rec.shape[1]
    n = w_out.shape[1]
    assert ka == kr
    return pl.pallas_call(
        _out_proj_kernel,
        grid=(m // tm, n // tn),
        in_specs=[
            pl.BlockSpec((tm, ka), lambda i, j: (i, 0)),
            pl.BlockSpec((tm, kr), lambda i, j: (i, 0)),
            pl.BlockSpec((ka, tn), lambda i, j: (0, j)),
            pl.BlockSpec((kr, tn), lambda i, j: (1, j)),
            pl.BlockSpec((tm, tn), lambda i, j: (i, j)),
        ],
        out_specs=pl.BlockSpec((tm, tn), lambda i, j: (i, j)),
        out_shape=jax.ShapeDtypeStruct((m, n), F32),
        compiler_params=_params("arbitrary", "arbitrary"),
        name="out_proj",
    )(attn, rec, w_out, w_out, x)


def _down_norm_kernel(u_ref, w_ref, x_ref, g_ref, o_ref):
    k = pl.program_id(1)

    @pl.when(k == 0)
    def _():
        o_ref[...] = x_ref[...]

    o_ref[...] = jnp.dot(u_ref[...], w_ref[...], preferred_element_type=F32) + o_ref[...]

    @pl.when(k == pl.num_programs(1) - 1)
    def _():
        o_ref[...] = _rmsnorm_rows(o_ref[...], g_ref[...])


def _down_norm(u, w, x, g, *, tm, tk):
    m, kk = u.shape
    n = w.shape[1]
    return pl.pallas_call(
        _down_norm_kernel,
        grid=(m // tm, kk // tk),
        in_specs=[
            pl.BlockSpec((tm, tk), lambda i, k: (i, k)),
            pl.BlockSpec((tk, n), lambda i, k: (k, 0)),
            pl.BlockSpec((tm, n), lambda i, k: (i, 0), pipeline_mode=pl.Buffered(1)),
            pl.BlockSpec((1, n), lambda i, k: (0, 0)),
        ],
        out_specs=pl.BlockSpec((tm, n), lambda i, k: (i, 0)),
        out_shape=jax.ShapeDtypeStruct((m, n), F32),
        compiler_params=_params("arbitrary", "arbitrary"),
        name="down_norm",
    )(u, w, x, g)


def kernel(x, norm_mix_g, w_in, conv_w, conv_b, w_rgate, b_rgate, w_igate, b_igate, lru_lambda, lambda_q1, lambda_k1, lambda_q2, lambda_k2, subln_g, w_out, norm_mlp_g, w_mlp_up, w_mlp_down, norm_final_g):
    batch, seq, d_model = x.shape
    depth = w_in.shape[0]
    d_attn = ATTN_HEADS * HEAD_V_DIM
    d_lru = LRU_BLOCKS * LRU_BLOCK_DIM
    m = batch * seq
    row = lambda v: v.reshape(1, -1)

    xf = x.reshape(m, d_model)
    for l in range(depth):
        lambda_init = 0.8 - 0.6 * math.exp(-0.3 * l)
        w_in_l = w_in[l].astype(BF16)
        g_mix = row(norm_mix_g[l])
        qkv = _norm_matmul(xf, g_mix, w_in_l[:, :3 * d_attn], out_dtype=BF16, relu2=False,
                           scaled_cols=d_attn, scale=ATTN_Q_PRESCALE, **IN_PROJ_TILES)

        lam = _diff_lambda(row(lambda_q1[l]), row(lambda_k1[l]), row(lambda_q2[l]),
                           row(lambda_k2[l]), lambda_init)
        attn, w_out_l, w_up_l, w_down_l = _diff_attention(
            qkv.reshape(batch, seq, 3 * d_attn), lam, subln_g[l].reshape(HEAD_V_DIM, 1),
            (w_out[l], w_mlp_up[l], w_mlp_down[l]),
            batch=batch, seq=seq, out_scale=1.0 - lambda_init)

        rec = _rec_branch(
            xf, g_mix, w_in_l, 3 * d_attn, 3 * d_attn + d_lru, conv_w[l], row(conv_b[l]),
            w_rgate[l].astype(BF16), b_rgate[l].reshape(LRU_BLOCKS, 1, LRU_BLOCK_DIM),
            w_igate[l].astype(BF16), b_igate[l].reshape(LRU_BLOCKS, 1, LRU_BLOCK_DIM),
            row(lru_lambda[l]), seq=seq)

        xf = _out_proj(attn.reshape(m, d_attn), rec, w_out_l, xf, **OUT_PROJ_TILES)

        u = _norm_matmul(xf, row(norm_mlp_g[l]), w_up_l, out_dtype=BF16, relu2=True,
                         **UP_PROJ_TILES)
        last = l == depth - 1
        assert last, "DEPTH > 1 needs an un-normalised down-projection variant"
        xf = _down_norm(u, w_down_l, xf, row(norm_final_g), **DOWN_PROJ_TILES)
    return xf.reshape(batch, seq, d_model)
```

```python
import functools
import math

import jax
import jax.numpy as jnp
from jax import lax
from jax.experimental import pallas as pl
from jax.experimental.pallas import tpu as pltpu

F32 = jnp.float32
BF16 = jnp.bfloat16

EPS = 1e-6
CHUNK = 64
ATTN_HEADS = 16
HEAD_V_DIM = 128
HEAD_QK_DIM = 64
LRU_BLOCKS = 16
LRU_BLOCK_DIM = 128
CONV_WIDTH = 4
LRU_C = 8.0

V7X_VMEM_LIMIT_BYTES = 56 * 1024 * 1024
SUBLANES = 8
BF16_ROW_TILE = 16

IN_PROJ_TILES = dict(tm=512, tn=1024)
OUT_PROJ_TILES = dict(tm=1024, tn=512)
UP_PROJ_TILES = dict(tm=512, tn=1024)
DOWN_PROJ_TILES = dict(tm=512, tk=1024)
ATTN_TILE = 512
ATTN_HEADS_PER_STEP = 4
LRU_TILE = 512
LRU_CHANNEL_TILE = 256


def _params(*semantics):
    return pltpu.CompilerParams(dimension_semantics=semantics,
                                vmem_limit_bytes=V7X_VMEM_LIMIT_BYTES)


def _rmsnorm_rows(x, g):
    ms = jnp.mean(x * x, axis=-1, keepdims=True)
    return (x * lax.rsqrt(ms + EPS)) * g


def _norm_matmul_kernel(x_ref, g_ref, w_ref, o_ref, h_ref, *, relu2, scaled_tiles, scale):
    @pl.when(pl.program_id(1) == 0)
    def _():
        h_ref[...] = _rmsnorm_rows(x_ref[...], g_ref[...]).astype(BF16)

    acc = jnp.dot(h_ref[...], w_ref[...], preferred_element_type=F32)
    if relu2:
        acc = jnp.square(jnp.maximum(acc, 0.0))
    if scaled_tiles:
        acc = acc * jnp.where(pl.program_id(1) < scaled_tiles, scale, 1.0).astype(F32)
    o_ref[...] = acc.astype(o_ref.dtype)


def _norm_matmul(x, g, w, *, out_dtype, relu2, tm, tn, n_cols=None, scaled_cols=0, scale=1.0):
    m, k = x.shape
    n = w.shape[1] if n_cols is None else n_cols
    assert scaled_cols % tn == 0 and n % tn == 0
    return pl.pallas_call(
        functools.partial(_norm_matmul_kernel, relu2=relu2, scaled_tiles=scaled_cols // tn,
                          scale=scale),
        grid=(m // tm, n // tn),
        in_specs=[
            pl.BlockSpec((tm, k), lambda i, j: (i, 0)),
            pl.BlockSpec((1, k), lambda i, j: (0, 0)),
            pl.BlockSpec((k, tn), lambda i, j: (0, j)),
        ],
        out_specs=pl.BlockSpec((tm, tn), lambda i, j: (i, j)),
        out_shape=jax.ShapeDtypeStruct((m, n), out_dtype),
        scratch_shapes=[pltpu.VMEM((tm, k), BF16)],
        compiler_params=_params("arbitrary", "arbitrary"),
        name="norm_matmul",
    )(x, g, w)


def _lambda_kernel(q1_ref, k1_ref, q2_ref, k2_ref, o_ref, *, lambda_init):
    d1 = jnp.sum(q1_ref[...] * k1_ref[...], axis=-1, keepdims=True)
    d2 = jnp.sum(q2_ref[...] * k2_ref[...], axis=-1, keepdims=True)
    o_ref[...] = jnp.exp(d1) - jnp.exp(d2) + lambda_init


def _diff_lambda(q1, k1, q2, k2, lambda_init):
    return pl.pallas_call(
        functools.partial(_lambda_kernel, lambda_init=lambda_init),
        out_shape=jax.ShapeDtypeStruct((1, 1), F32),
        name="diff_lambda",
    )(q1, k1, q2, k2)


ATTN_ONES_ROWS = 16
ATTN_Q_PRESCALE = HEAD_QK_DIM ** -0.5 * math.log2(math.e)


def _attn_kernel(lam_ref, g_ref, q_ref, k_ref, v_ref, *refs, tq, n_tiles, heads, out_scale,
                 n_cast):
    cast_in, o_ref, cast_out = refs[:n_cast], refs[n_cast], refs[n_cast + 1:2 * n_cast + 1]
    vt_ref, qm_ref, acc_ref, m_ref, s_ref = refs[2 * n_cast + 1:]
    qi = pl.program_id(2)
    dv = HEAD_V_DIM
    head_lanes = lambda h: slice(h * dv, (h + 1) * dv)

    for w_ref, wb_ref in zip(cast_in, cast_out):
        wb_ref[...] = w_ref[...].astype(BF16)

    @pl.when(qi == 0)
    def _():
        for h in range(heads):
            for j in range(n_tiles):
                vt = v_ref[0, j * tq:(j + 1) * tq, head_lanes(h)].astype(F32).T
                vt_ref[h, j, 0:dv, :] = vt.astype(BF16)
                vt_ref[h, j, dv:dv + ATTN_ONES_ROWS, :] = jnp.ones((ATTN_ONES_ROWS, tq), BF16)

    lane = lax.broadcasted_iota(jnp.int32, (tq, dv), 1)
    for h in range(heads):
        q = q_ref[0, :, head_lanes(h)]
        zero = jnp.zeros_like(q)
        qm_ref[h, 0] = jnp.where(lane < HEAD_QK_DIM, q, zero)
        qm_ref[h, 1] = jnp.where(lane >= HEAD_QK_DIM, q, zero)
    acc_ref[...] = jnp.zeros_like(acc_ref)
    m_ref[...] = jnp.full(m_ref.shape, -jnp.inf, F32)
    nt = (((1,), (1,)), ((), ()))

    def scores_into(s_ref, j, h):
        k = k_ref[0, pl.ds(pl.multiple_of(j * tq, tq), tq), head_lanes(h)]
        for mi in range(2):
            s_ref[mi] = lax.dot_general(k, qm_ref[h, mi], nt, preferred_element_type=F32)

    def softmax_pv(s_ref, j, h, masked):
        vt = vt_ref[h, j]
        if masked:
            key_chunk = lax.broadcasted_iota(jnp.int32, (tq, tq), 0) // CHUNK
            qry_chunk = lax.broadcasted_iota(jnp.int32, (tq, tq), 1) // CHUNK
            visible = key_chunk <= qry_chunk
            read = lambda mi: jnp.where(visible, s_ref[mi], -jnp.inf)
        else:
            read = lambda mi: s_ref[mi]
        for mi in range(2):
            m_old = m_ref[h, mi, 0:1, :]
            m_new = jnp.maximum(m_old, jnp.max(read(mi), axis=0, keepdims=True))
            alpha = jnp.exp2(m_old - m_new)
            p = jnp.exp2(read(mi) - m_new).astype(BF16)
            pv = jnp.dot(vt, p, preferred_element_type=F32)
            acc_ref[h, mi] = alpha * acc_ref[h, mi] + pv
            m_ref[h, mi] = jnp.broadcast_to(m_new, m_ref.shape[2:])

    scores_into(s_ref.at[0], 0, 0)

    def one_tile(j, masked):
        for h in range(heads):
            if h + 1 < heads:
                scores_into(s_ref.at[h + 1], j, h + 1)
            elif not masked:
                scores_into(s_ref.at[0], j + 1, 0)
            softmax_pv(s_ref.at[h], j, h, masked)

    def body(j, carry):
        one_tile(j, False)
        return carry

    lax.fori_loop(0, qi, body, 0)
    one_tile(qi, True)

    lam = lam_ref[0, 0]
    for h in range(heads):
        o1 = acc_ref[h, 0, 0:dv, :] * (1.0 / acc_ref[h, 0, dv:dv + 1, :])
        o2 = acc_ref[h, 1, 0:dv, :] * (lam / acc_ref[h, 1, dv:dv + 1, :])
        a = o1 - o2
        ms = jnp.mean(a * a, axis=0, keepdims=True)
        y = ((a * lax.rsqrt(ms + EPS)) * g_ref[...]) * out_scale
        o_ref[0, :, head_lanes(h)] = y.T.astype(o_ref.dtype)


def _diff_attention(qkv, lam, subln_g_col, cast_weights=(), *, batch, seq, out_scale,
                    tq=ATTN_TILE, heads=ATTN_HEADS_PER_STEP):
    assert ATTN_HEADS % heads == 0
    groups = ATTN_HEADS // heads
    n_tiles = seq // tq
    acc_rows = HEAD_V_DIM + ATTN_ONES_ROWS
    width = heads * HEAD_V_DIM
    n_steps = batch * groups * n_tiles
    step = lambda b, g, i: (b * groups + g) * n_tiles + i
    cast_specs = []
    for w in cast_weights:
        assert w.shape[0] % (n_steps * BF16_ROW_TILE) == 0
        cast_specs.append(pl.BlockSpec((w.shape[0] // n_steps, w.shape[1]),
                                       lambda b, g, i: (step(b, g, i), 0)))
    return pl.pallas_call(
        functools.partial(_attn_kernel, tq=tq, n_tiles=n_tiles, heads=heads, out_scale=out_scale,
                          n_cast=len(cast_weights)),
        grid=(batch, groups, n_tiles),
        in_specs=[
            pl.BlockSpec(memory_space=pltpu.SMEM),
            pl.BlockSpec((HEAD_V_DIM, 1), lambda b, g, i: (0, 0)),
            pl.BlockSpec((1, tq, width), lambda b, g, i: (b, i, g)),
            pl.BlockSpec((1, seq, width), lambda b, g, i: (b, 0, groups + g)),
            pl.BlockSpec((1, seq, width), lambda b, g, i: (b, 0, 2 * groups + g)),
        ] + cast_specs,
        out_specs=[pl.BlockSpec((1, tq, width), lambda b, g, i: (b, i, g))] + cast_specs,
        out_shape=[jax.ShapeDtypeStruct((batch, seq, ATTN_HEADS * HEAD_V_DIM), BF16)]
        + [jax.ShapeDtypeStruct(w.shape, BF16) for w in cast_weights],
        scratch_shapes=[pltpu.VMEM((heads, n_tiles, acc_rows, tq), BF16),
                        pltpu.VMEM((heads, 2, tq, HEAD_V_DIM), BF16),
                        pltpu.VMEM((heads, 2, acc_rows, tq), F32),
                        pltpu.VMEM((heads, 2, SUBLANES, tq), F32),
                        pltpu.VMEM((heads, 2, tq, tq), F32)],
        compiler_params=_params("arbitrary", "arbitrary", "arbitrary"),
        name="diff_attention",
    )(lam, subln_g_col, qkv, qkv, qkv, *cast_weights)


def _causal_conv(x, tail, cw, cb):
    ts = x.shape[0]
    xp = jnp.concatenate([tail, x], axis=0)
    xc = cb
    for j in range(CONV_WIDTH):
        off = SUBLANES - (CONV_WIDTH - 1) + j
        xc = xc + cw[j:j + 1, :] * xp[off:off + ts]
    return xc


def _block_gate(xb, w_ref, b_ref):
    blocks = [jnp.dot(xb[:, n * LRU_BLOCK_DIM:(n + 1) * LRU_BLOCK_DIM], w_ref[n],
                      preferred_element_type=F32) + b_ref[n]
              for n in range(xb.shape[1] // LRU_BLOCK_DIM)]
    return jax.nn.sigmoid(jnp.concatenate(blocks, axis=1))


def _lru_scan(xc, r, i, gate_act, h0, lam):
    ts, c = xc.shape
    log_sig = jnp.minimum(lam, 0.0) - jnp.log1p(jnp.exp(-jnp.abs(lam)))
    log_a = LRU_C * r * log_sig
    a = jnp.exp(log_a)
    b = jnp.sqrt(-jnp.tanh(log_a) * (1.0 + a * a)) * (i * xc)

    groups = ts // SUBLANES
    a = a.reshape(groups, SUBLANES, c)
    b = b.reshape(groups, SUBLANES, c)
    frame = lax.broadcasted_iota(jnp.int32, (1, SUBLANES, c), 1)
    d = 1
    while d < SUBLANES:
        keep = frame >= d
        b = b + a * jnp.where(keep, pltpu.roll(b, d, axis=1), 0.0)
        a = a * jnp.where(keep, pltpu.roll(a, d, axis=1), 1.0)
        d *= 2
    carry = h0
    hs = []
    for gi in range(groups):
        hg = b[gi] + a[gi] * carry
        carry = hg[SUBLANES - 1:SUBLANES, :]
        hs.append(hg)
    return jnp.concatenate(hs, axis=0) * gate_act, carry


def _rec_branch_kernel(x_ref, g_ref, wx_ref, wg_ref, cw_ref, cb_ref, wr_ref, br_ref, wi_ref,
                       bi_ref, lam_ref, o_ref, h_ref, xr_ref, gr_ref, tail_ref, state_ref,
                       *, nj, tiles_per_seq, n_tiles):
    s = pl.program_id(0)
    j = lax.rem(jnp.minimum(s, n_tiles - 1), nj)
    prev = jnp.maximum(s - 1, 0)
    ip = prev // nj
    jp = lax.rem(prev, nj)
    ts = xr_ref.shape[0]

    @pl.when(s == 0)
    def _():
        xr_ref[...] = jnp.zeros_like(xr_ref)
        gr_ref[...] = jnp.zeros_like(gr_ref)
        tail_ref[...] = jnp.zeros_like(tail_ref)
        state_ref[...] = jnp.zeros_like(state_ref)

    @pl.when(j == 0)
    def _():
        h_ref[...] = _rmsnorm_rows(x_ref[...], g_ref[...]).astype(BF16)

    seq_start = lax.rem(ip, tiles_per_seq) == 0
    tail = jnp.where(seq_start, 0.0, tail_ref[jp])
    h0 = jnp.where(seq_start, 0.0, state_ref[jp, 0:1, :])
    x = xr_ref[...]
    gate_act = jax.nn.gelu(gr_ref[...], approximate=True)
    xc = _causal_conv(x, tail, cw_ref[...], cb_ref[...])
    xr_ref[...] = jnp.dot(h_ref[...], wx_ref[...], preferred_element_type=F32)
    xb = xc.astype(BF16)
    r = _block_gate(xb, wr_ref, br_ref)
    i = _block_gate(xb, wi_ref, bi_ref)
    gr_ref[...] = jnp.dot(h_ref[...], wg_ref[...], preferred_element_type=F32)
    rec, h_last = _lru_scan(xc, r, i, gate_act, h0, lam_ref[...])
    o_ref[...] = rec.astype(o_ref.dtype)
    tail_ref[jp] = x[ts - SUBLANES:]
    state_ref[jp] = jnp.broadcast_to(h_last, state_ref.shape[1:])


def _rec_branch(x, g, w_in, col_x, col_g, conv_w, conv_b, w_r, b_r, w_i, b_i, lam, *, seq,
                tm=LRU_TILE, tc=LRU_CHANNEL_TILE):
    m, k = x.shape
    d_lru = LRU_BLOCKS * LRU_BLOCK_DIM
    assert seq % tm == 0 and d_lru % tc == 0 and col_x % tc == 0 and col_g % tc == 0
    nj = d_lru // tc
    n_tiles = (m // tm) * nj
    nblk = tc // LRU_BLOCK_DIM
    this_i = lambda s: jnp.minimum(s, n_tiles - 1) // nj
    this_j = lambda s: lax.rem(jnp.minimum(s, n_tiles - 1), nj)
    prev_i = lambda s: jnp.maximum(s - 1, 0) // nj
    prev_j = lambda s: lax.rem(jnp.maximum(s - 1, 0), nj)
    vec = lambda s: (0, prev_j(s))
    blk = lambda s: (prev_j(s), 0, 0)
    return pl.pallas_call(
        functools.partial(_rec_branch_kernel, nj=nj, tiles_per_seq=seq // tm, n_tiles=n_tiles),
        grid=(n_tiles + 1,),
        in_specs=[
            pl.BlockSpec((tm, k), lambda s: (this_i(s), 0)),
            pl.BlockSpec((1, k), lambda s: (0, 0)),
            pl.BlockSpec((k, tc), lambda s: (0, col_x // tc + this_j(s))),
            pl.BlockSpec((k, tc), lambda s: (0, col_g // tc + this_j(s))),
            pl.BlockSpec((CONV_WIDTH, tc), vec),
            pl.BlockSpec((1, tc), vec),
            pl.BlockSpec((nblk, LRU_BLOCK_DIM, LRU_BLOCK_DIM), blk),
            pl.BlockSpec((nblk, 1, LRU_BLOCK_DIM), blk),
            pl.BlockSpec((nblk, LRU_BLOCK_DIM, LRU_BLOCK_DIM), blk),
            pl.BlockSpec((nblk, 1, LRU_BLOCK_DIM), blk),
            pl.BlockSpec((1, tc), vec),
        ],
        out_specs=pl.BlockSpec((tm, tc), lambda s: (prev_i(s), prev_j(s))),
        out_shape=jax.ShapeDtypeStruct((m, d_lru), BF16),
        scratch_shapes=[pltpu.VMEM((tm, k), BF16),
                        pltpu.VMEM((tm, tc), F32), pltpu.VMEM((tm, tc), F32),
                        pltpu.VMEM((nj, SUBLANES, tc), F32),
                        pltpu.VMEM((nj, SUBLANES, tc), F32)],
        compiler_params=_params("arbitrary"),
        name="rec_branch",
    )(x, g, w_in, w_in, conv_w, conv_b, w_r, b_r, w_i, b_i, lam)


def _out_proj_kernel(a_ref, r_ref, wa_ref, wr_ref, x_ref, o_ref):
    acc = jnp.dot(a_ref[...], wa_ref[...], preferred_element_type=F32) + x_ref[...]
    o_ref[...] = jnp.dot(r_ref[...], wr_ref[...], preferred_element_type=F32) + acc


def _out_proj(attn, rec, w_out, x, *, tm, tn):
    m, ka = attn.shape
    kr = rec.shape[1]
    n = w_out.shape[1]
    assert ka == kr
    return pl.pallas_call(
        _out_proj_kernel,
        grid=(m // tm, n // tn),
        in_specs=[
            pl.BlockSpec((tm, ka), lambda i, j: (i, 0)),
            pl.BlockSpec((tm, kr), lambda i, j: (i, 0)),
            pl.BlockSpec((ka, tn), lambda i, j: (0, j)),
            pl.BlockSpec((kr, tn), lambda i, j: (1, j)),
            pl.BlockSpec((tm, tn), lambda i, j: (i, j)),
        ],
        out_specs=pl.BlockSpec((tm, tn), lambda i, j: (i, j)),
        out_shape=jax.ShapeDtypeStruct((m, n), F32),
        compiler_params=_params("arbitrary", "arbitrary"),
        name="out_proj",
    )(attn, rec, w_out, w_out, x)


def _down_norm_kernel(u_ref, w_ref, x_ref, g_ref, o_ref):
    k = pl.program_id(1)

    @pl.when(k == 0)
    def _():
        o_ref[...] = x_ref[...]

    o_ref[...] = jnp.dot(u_ref[...], w_ref[...], preferred_element_type=F32) + o_ref[...]

    @pl.when(k == pl.num_programs(1) - 1)
    def _():
        o_ref[...] = _rmsnorm_rows(o_ref[...], g_ref[...])


def _down_norm(u, w, x, g, *, tm, tk):
    m, kk = u.shape
    n = w.shape[1]
    return pl.pallas_call(
        _down_norm_kernel,
        grid=(m // tm, kk // tk),
        in_specs=[
            pl.BlockSpec((tm, tk), lambda i, k: (i, k)),
            pl.BlockSpec((tk, n), lambda i, k: (k, 0)),
            pl.BlockSpec((tm, n), lambda i, k: (i, 0), pipeline_mode=pl.Buffered(1)),
            pl.BlockSpec((1, n), lambda i, k: (0, 0)),
        ],
        out_specs=pl.BlockSpec((tm, n), lambda i, k: (i, 0)),
        out_shape=jax.ShapeDtypeStruct((m, n), F32),
        compiler_params=_params("arbitrary", "arbitrary"),
        name="down_norm",
    )(u, w, x, g)


def kernel(x, norm_mix_g, w_in, conv_w, conv_b, w_rgate, b_rgate, w_igate, b_igate, lru_lambda, lambda_q1, lambda_k1, lambda_q2, lambda_k2, subln_g, w_out, norm_mlp_g, w_mlp_up, w_mlp_down, norm_final_g):
    batch, seq, d_model = x.shape
    depth = w_in.shape[0]
    d_attn = ATTN_HEADS * HEAD_V_DIM
    d_lru = LRU_BLOCKS * LRU_BLOCK_DIM
    m = batch * seq
    row = lambda v: v.reshape(1, -1)

    xf = x.reshape(m, d_model)
    for l in range(depth):
        lambda_init = 0.8 - 0.6 * math.exp(-0.3 * l)
        w_in_l = w_in[l].astype(BF16)
        g_mix = row(norm_mix_g[l])
        qkv = _norm_matmul(xf, g_mix, w_in_l, out_dtype=BF16, relu2=False, n_cols=3 * d_attn,
                           scaled_cols=d_attn, scale=ATTN_Q_PRESCALE, **IN_PROJ_TILES)

        lam = _diff_lambda(row(lambda_q1[l]), row(lambda_k1[l]), row(lambda_q2[l]),
                           row(lambda_k2[l]), lambda_init)
        attn, w_out_l, w_up_l, w_down_l = _diff_attention(
            qkv.reshape(batch, seq, 3 * d_attn), lam, subln_g[l].reshape(HEAD_V_DIM, 1),
            (w_out[l], w_mlp_up[l], w_mlp_down[l]),
            batch=batch, seq=seq, out_scale=1.0 - lambda_init)

        rec = _rec_branch(
            xf, g_mix, w_in_l, 3 * d_attn, 3 * d_attn + d_lru, conv_w[l], row(conv_b[l]),
            w_rgate[l].astype(BF16), b_rgate[l].reshape(LRU_BLOCKS, 1, LRU_BLOCK_DIM),
            w_igate[l].astype(BF16), b_igate[l].reshape(LRU_BLOCKS, 1, LRU_BLOCK_DIM),
            row(lru_lambda[l]), seq=seq)

        xf = _out_proj(attn.reshape(m, d_attn), rec, w_out_l, xf, **OUT_PROJ_TILES)

        u = _norm_matmul(xf, row(norm_mlp_g[l]), w_up_l, out_dtype=BF16, relu2=True,
                         **UP_PROJ_TILES)
        last = l == depth - 1
        assert last, "DEPTH > 1 needs an un-normalised down-projection variant"
        xf = _down_norm(u, w_down_l, xf, row(norm_final_g), **DOWN_PROJ_TILES)
    return xf.reshape(batch, seq, d_model)
```

```python
import functools
import math

import jax
import jax.numpy as jnp
from jax import lax
from jax.experimental import pallas as pl
from jax.experimental.pallas import tpu as pltpu

F32 = jnp.float32
BF16 = jnp.bfloat16

EPS = 1e-6
CHUNK = 64
ATTN_HEADS = 16
HEAD_V_DIM = 128
HEAD_QK_DIM = 64
LRU_BLOCKS = 16
LRU_BLOCK_DIM = 128
CONV_WIDTH = 4
LRU_C = 8.0

V7X_VMEM_LIMIT_BYTES = 56 * 1024 * 1024
SUBLANES = 8
BF16_ROW_TILE = 16

IN_PROJ_TILES = dict(tm=512, tn=1024)
OUT_PROJ_TILES = dict(tm=1024, tn=512)
UP_PROJ_TILES = dict(tm=512, tn=1024)
DOWN_PROJ_TILES = dict(tm=512, tk=1024)
ATTN_TILE = 512
ATTN_HEADS_PER_STEP = 4
LRU_TILE = 512
LRU_CHANNEL_TILE = 256


def _params(*semantics):
    return pltpu.CompilerParams(dimension_semantics=semantics,
                                vmem_limit_bytes=V7X_VMEM_LIMIT_BYTES)


def _rmsnorm_rows(x, g):
    ms = jnp.mean(x * x, axis=-1, keepdims=True)
    return (x * lax.rsqrt(ms + EPS)) * g


def _norm_matmul_kernel(x_ref, g_ref, w_ref, o_ref, h_ref, *, relu2, scaled_tiles, scale):
    @pl.when(pl.program_id(1) == 0)
    def _():
        h_ref[...] = _rmsnorm_rows(x_ref[...], g_ref[...]).astype(BF16)

    acc = jnp.dot(h_ref[...], w_ref[...], preferred_element_type=F32)
    if relu2:
        acc = jnp.square(jnp.maximum(acc, 0.0))
    if scaled_tiles:
        acc = acc * jnp.where(pl.program_id(1) < scaled_tiles, scale, 1.0).astype(F32)
    o_ref[...] = acc.astype(o_ref.dtype)


def _norm_matmul(x, g, w, *, out_dtype, relu2, tm, tn, n_cols=None, scaled_cols=0, scale=1.0,
                 return_normed=False):
    m, k = x.shape
    n = w.shape[1] if n_cols is None else n_cols
    assert scaled_cols % tn == 0 and n % tn == 0
    out_specs = [pl.BlockSpec((tm, tn), lambda i, j: (i, j))]
    out_shape = [jax.ShapeDtypeStruct((m, n), out_dtype)]
    scratch = [pltpu.VMEM((tm, k), BF16)]
    if return_normed:
        out_specs.append(pl.BlockSpec((tm, k), lambda i, j: (i, 0)))
        out_shape.append(jax.ShapeDtypeStruct((m, k), BF16))
        scratch = []
    out = pl.pallas_call(
        functools.partial(_norm_matmul_kernel, relu2=relu2, scaled_tiles=scaled_cols // tn,
                          scale=scale),
        grid=(m // tm, n // tn),
        in_specs=[
            pl.BlockSpec((tm, k), lambda i, j: (i, 0)),
            pl.BlockSpec((1, k), lambda i, j: (0, 0)),
            pl.BlockSpec((k, tn), lambda i, j: (0, j)),
        ],
        out_specs=out_specs,
        out_shape=out_shape,
        scratch_shapes=scratch,
        compiler_params=_params("arbitrary", "arbitrary"),
        name="norm_matmul",
    )(x, g, w)
    return out if return_normed else out[0]


def _lambda_kernel(q1_ref, k1_ref, q2_ref, k2_ref, o_ref, *, lambda_init):
    d1 = jnp.sum(q1_ref[...] * k1_ref[...], axis=-1, keepdims=True)
    d2 = jnp.sum(q2_ref[...] * k2_ref[...], axis=-1, keepdims=True)
    o_ref[...] = jnp.exp(d1) - jnp.exp(d2) + lambda_init


def _diff_lambda(q1, k1, q2, k2, lambda_init):
    return pl.pallas_call(
        functools.partial(_lambda_kernel, lambda_init=lambda_init),
        out_shape=jax.ShapeDtypeStruct((1, 1), F32),
        name="diff_lambda",
    )(q1, k1, q2, k2)


ATTN_ONES_ROWS = 16
ATTN_Q_PRESCALE = HEAD_QK_DIM ** -0.5 * math.log2(math.e)


def _attn_kernel(lam_ref, g_ref, q_ref, k_ref, v_ref, *refs, tq, n_tiles, heads, out_scale,
                 n_cast):
    cast_in, o_ref, cast_out = refs[:n_cast], refs[n_cast], refs[n_cast + 1:2 * n_cast + 1]
    vt_ref, qm_ref, acc_ref, m_ref, s_ref = refs[2 * n_cast + 1:]
    qi = pl.program_id(2)
    dv = HEAD_V_DIM
    head_lanes = lambda h: slice(h * dv, (h + 1) * dv)

    for w_ref, wb_ref in zip(cast_in, cast_out):
        wb_ref[...] = w_ref[...].astype(BF16)

    @pl.when(qi == 0)
    def _():
        for h in range(heads):
            for j in range(n_tiles):
                vt = v_ref[0, j * tq:(j + 1) * tq, head_lanes(h)].astype(F32).T
                vt_ref[h, j, 0:dv, :] = vt.astype(BF16)
                vt_ref[h, j, dv:dv + ATTN_ONES_ROWS, :] = jnp.ones((ATTN_ONES_ROWS, tq), BF16)

    lane = lax.broadcasted_iota(jnp.int32, (tq, dv), 1)
    for h in range(heads):
        q = q_ref[0, :, head_lanes(h)]
        zero = jnp.zeros_like(q)
        qm_ref[h, 0] = jnp.where(lane < HEAD_QK_DIM, q, zero)
        qm_ref[h, 1] = jnp.where(lane >= HEAD_QK_DIM, q, zero)
    acc_ref[...] = jnp.zeros_like(acc_ref)
    m_ref[...] = jnp.full(m_ref.shape, -jnp.inf, F32)
    nt = (((1,), (1,)), ((), ()))

    def scores_into(s_ref, j, h):
        k = k_ref[0, pl.ds(pl.multiple_of(j * tq, tq), tq), head_lanes(h)]
        for mi in range(2):
            s_ref[mi] = lax.dot_general(k, qm_ref[h, mi], nt, preferred_element_type=F32)

    def softmax_pv(s_ref, j, h, masked):
        vt = vt_ref[h, j]
        if masked:
            key_chunk = lax.broadcasted_iota(jnp.int32, (tq, tq), 0) // CHUNK
            qry_chunk = lax.broadcasted_iota(jnp.int32, (tq, tq), 1) // CHUNK
            visible = key_chunk <= qry_chunk
            read = lambda mi: jnp.where(visible, s_ref[mi], -jnp.inf)
        else:
            read = lambda mi: s_ref[mi]
        for mi in range(2):
            m_old = m_ref[h, mi, 0:1, :]
            m_new = jnp.maximum(m_old, jnp.max(read(mi), axis=0, keepdims=True))
            alpha = jnp.exp2(m_old - m_new)
            p = jnp.exp2(read(mi) - m_new).astype(BF16)
            pv = jnp.dot(vt, p, preferred_element_type=F32)
            acc_ref[h, mi] = alpha * acc_ref[h, mi] + pv
            m_ref[h, mi] = jnp.broadcast_to(m_new, m_ref.shape[2:])

    scores_into(s_ref.at[0], 0, 0)

    def one_tile(j, masked):
        for h in range(heads):
            if h + 1 < heads:
                scores_into(s_ref.at[h + 1], j, h + 1)
            elif not masked:
                scores_into(s_ref.at[0], j + 1, 0)
            softmax_pv(s_ref.at[h], j, h, masked)

    def two_tiles(t, carry):
        one_tile(2 * t, False)
        one_tile(2 * t + 1, False)
        return carry

    lax.fori_loop(0, lax.shift_right_logical(qi, 1), two_tiles, 0)

    @pl.when(lax.bitwise_and(qi, 1) == 1)
    def _():
        one_tile(qi - 1, False)

    one_tile(qi, True)

    lam = lam_ref[0, 0]
    for h in range(heads):
        o1 = acc_ref[h, 0, 0:dv, :] * (1.0 / acc_ref[h, 0, dv:dv + 1, :])
        o2 = acc_ref[h, 1, 0:dv, :] * (lam / acc_ref[h, 1, dv:dv + 1, :])
        a = o1 - o2
        ms = jnp.mean(a * a, axis=0, keepdims=True)
        y = ((a * lax.rsqrt(ms + EPS)) * g_ref[...]) * out_scale
        o_ref[0, :, head_lanes(h)] = y.T.astype(o_ref.dtype)


def _diff_attention(qkv, lam, subln_g_col, cast_weights=(), *, batch, seq, out_scale,
                    tq=ATTN_TILE, heads=ATTN_HEADS_PER_STEP):
    assert ATTN_HEADS % heads == 0
    groups = ATTN_HEADS // heads
    n_tiles = seq // tq
    acc_rows = HEAD_V_DIM + ATTN_ONES_ROWS
    width = heads * HEAD_V_DIM
    n_steps = batch * groups * n_tiles
    step = lambda b, g, i: (b * groups + g) * n_tiles + i
    cast_specs = []
    for w in cast_weights:
        assert w.shape[0] % (n_steps * BF16_ROW_TILE) == 0
        cast_specs.append(pl.BlockSpec((w.shape[0] // n_steps, w.shape[1]),
                                       lambda b, g, i: (step(b, g, i), 0)))
    return pl.pallas_call(
        functools.partial(_attn_kernel, tq=tq, n_tiles=n_tiles, heads=heads, out_scale=out_scale,
                          n_cast=len(cast_weights)),
        grid=(batch, groups, n_tiles),
        in_specs=[
            pl.BlockSpec(memory_space=pltpu.SMEM),
            pl.BlockSpec((HEAD_V_DIM, 1), lambda b, g, i: (0, 0)),
            pl.BlockSpec((1, tq, width), lambda b, g, i: (b, i, g)),
            pl.BlockSpec((1, seq, width), lambda b, g, i: (b, 0, groups + g)),
            pl.BlockSpec((1, seq, width), lambda b, g, i: (b, 0, 2 * groups + g)),
        ] + cast_specs,
        out_specs=[pl.BlockSpec((1, tq, width), lambda b, g, i: (b, i, g))] + cast_specs,
        out_shape=[jax.ShapeDtypeStruct((batch, seq, ATTN_HEADS * HEAD_V_DIM), BF16)]
        + [jax.ShapeDtypeStruct(w.shape, BF16) for w in cast_weights],
        scratch_shapes=[pltpu.VMEM((heads, n_tiles, acc_rows, tq), BF16),
                        pltpu.VMEM((heads, 2, tq, HEAD_V_DIM), BF16),
                        pltpu.VMEM((heads, 2, acc_rows, tq), F32),
                        pltpu.VMEM((heads, 2, SUBLANES, tq), F32),
                        pltpu.VMEM((heads, 2, tq, tq), F32)],
        compiler_params=_params("arbitrary", "arbitrary", "arbitrary"),
        name="diff_attention",
    )(lam, subln_g_col, qkv, qkv, qkv, *cast_weights)


def _causal_conv(x, tail, cw, cb):
    ts = x.shape[0]
    xp = jnp.concatenate([tail, x], axis=0)
    xc = cb
    for j in range(CONV_WIDTH):
        off = SUBLANES - (CONV_WIDTH - 1) + j
        xc = xc + cw[j:j + 1, :] * xp[off:off + ts]
    return xc


def _block_gate(xb, w_ref, b_ref):
    blocks = [jnp.dot(xb[:, n * LRU_BLOCK_DIM:(n + 1) * LRU_BLOCK_DIM], w_ref[n],
                      preferred_element_type=F32) + b_ref[n]
              for n in range(xb.shape[1] // LRU_BLOCK_DIM)]
    return jax.nn.sigmoid(jnp.concatenate(blocks, axis=1))


def _lru_scan(xc, r, i, gate_act, h0, lam):
    ts, c = xc.shape
    log_sig = jnp.minimum(lam, 0.0) - jnp.log1p(jnp.exp(-jnp.abs(lam)))
    log_a = LRU_C * r * log_sig
    a = jnp.exp(log_a)
    b = jnp.sqrt(-jnp.tanh(log_a) * (1.0 + a * a)) * (i * xc)

    groups = ts // SUBLANES
    a = a.reshape(groups, SUBLANES, c)
    b = b.reshape(groups, SUBLANES, c)
    frame = lax.broadcasted_iota(jnp.int32, (1, SUBLANES, c), 1)
    d = 1
    while d < SUBLANES:
        keep = frame >= d
        b = b + a * jnp.where(keep, pltpu.roll(b, d, axis=1), 0.0)
        a = a * jnp.where(keep, pltpu.roll(a, d, axis=1), 1.0)
        d *= 2
    carry = h0
    hs = []
    for gi in range(groups):
        hg = b[gi] + a[gi] * carry
        carry = hg[SUBLANES - 1:SUBLANES, :]
        hs.append(hg)
    return jnp.concatenate(hs, axis=0) * gate_act, carry


def _rec_branch_kernel(h_ref, wx_ref, wg_ref, cw_ref, cb_ref, wr_ref, br_ref, wi_ref,
                       bi_ref, lam_ref, o_ref, xr_ref, gr_ref, tail_ref, state_ref,
                       *, nj, tiles_per_seq):
    s = pl.program_id(0)
    prev = jnp.maximum(s - 1, 0)
    ip = prev // nj
    jp = lax.rem(prev, nj)
    ts = xr_ref.shape[0]

    @pl.when(s == 0)
    def _():
        xr_ref[...] = jnp.zeros_like(xr_ref)
        gr_ref[...] = jnp.zeros_like(gr_ref)
        tail_ref[...] = jnp.zeros_like(tail_ref)
        state_ref[...] = jnp.zeros_like(state_ref)

    seq_start = lax.rem(ip, tiles_per_seq) == 0
    tail = jnp.where(seq_start, 0.0, tail_ref[jp])
    h0 = jnp.where(seq_start, 0.0, state_ref[jp, 0:1, :])
    x = xr_ref[...]
    gate_act = jax.nn.gelu(gr_ref[...], approximate=True)
    xc = _causal_conv(x, tail, cw_ref[...], cb_ref[...])
    xr_ref[...] = jnp.dot(h_ref[...], wx_ref[...], preferred_element_type=F32)
    xb = xc.astype(BF16)
    r = _block_gate(xb, wr_ref, br_ref)
    i = _block_gate(xb, wi_ref, bi_ref)
    gr_ref[...] = jnp.dot(h_ref[...], wg_ref[...], preferred_element_type=F32)
    rec, h_last = _lru_scan(xc, r, i, gate_act, h0, lam_ref[...])
    o_ref[...] = rec.astype(o_ref.dtype)
    tail_ref[jp] = x[ts - SUBLANES:]
    state_ref[jp] = jnp.broadcast_to(h_last, state_ref.shape[1:])


def _rec_branch(h, w_in, col_x, col_g, conv_w, conv_b, w_r, b_r, w_i, b_i, lam, *, seq,
                tm=LRU_TILE, tc=LRU_CHANNEL_TILE):
    m, k = h.shape
    d_lru = LRU_BLOCKS * LRU_BLOCK_DIM
    assert seq % tm == 0 and d_lru % tc == 0 and col_x % tc == 0 and col_g % tc == 0
    nj = d_lru // tc
    n_tiles = (m // tm) * nj
    nblk = tc // LRU_BLOCK_DIM
    this_i = lambda s: jnp.minimum(s, n_tiles - 1) // nj
    this_j = lambda s: lax.rem(jnp.minimum(s, n_tiles - 1), nj)
    prev_i = lambda s: jnp.maximum(s - 1, 0) // nj
    prev_j = lambda s: lax.rem(jnp.maximum(s - 1, 0), nj)
    vec = lambda s: (0, prev_j(s))
    blk = lambda s: (prev_j(s), 0, 0)
    return pl.pallas_call(
        functools.partial(_rec_branch_kernel, nj=nj, tiles_per_seq=seq // tm),
        grid=(n_tiles + 1,),
        in_specs=[
            pl.BlockSpec((tm, k), lambda s: (this_i(s), 0)),
            pl.BlockSpec((k, tc), lambda s: (0, col_x // tc + this_j(s))),
            pl.BlockSpec((k, tc), lambda s: (0, col_g // tc + this_j(s))),
            pl.BlockSpec((CONV_WIDTH, tc), vec),
            pl.BlockSpec((1, tc), vec),
            pl.BlockSpec((nblk, LRU_BLOCK_DIM, LRU_BLOCK_DIM), blk),
            pl.BlockSpec((nblk, 1, LRU_BLOCK_DIM), blk),
            pl.BlockSpec((nblk, LRU_BLOCK_DIM, LRU_BLOCK_DIM), blk),
            pl.BlockSpec((nblk, 1, LRU_BLOCK_DIM), blk),
            pl.BlockSpec((1, tc), vec),
        ],
        out_specs=pl.BlockSpec((tm, tc), lambda s: (prev_i(s), prev_j(s))),
        out_shape=jax.ShapeDtypeStruct((m, d_lru), BF16),
        scratch_shapes=[pltpu.VMEM((tm, tc), F32), pltpu.VMEM((tm, tc), F32),
                        pltpu.VMEM((nj, SUBLANES, tc), F32),
                        pltpu.VMEM((nj, SUBLANES, tc), F32)],
        compiler_params=_params("arbitrary"),
        name="rec_branch",
    )(h, w_in, w_in, conv_w, conv_b, w_r, b_r, w_i, b_i, lam)


def _out_proj_kernel(a_ref, r_ref, wa_ref, wr_ref, x_ref, o_ref):
    acc = jnp.dot(a_ref[...], wa_ref[...], preferred_element_type=F32) + x_ref[...]
    o_ref[...] = jnp.dot(r_ref[...], wr_ref[...], preferred_element_type=F32) + acc


def _out_proj(attn, rec, w_out, x, *, tm, tn):
    m, ka = attn.shape
    kr = rec.shape[1]
    n = w_out.shape[1]
    assert ka == kr
    return pl.pallas_call(
        _out_proj_kernel,
        grid=(m // tm, n // tn),
        in_specs=[
            pl.BlockSpec((tm, ka), lambda i, j: (i, 0)),
            pl.BlockSpec((tm, kr), lambda i, j: (i, 0)),
            pl.BlockSpec((ka, tn), lambda i, j: (0, j)),
            pl.BlockSpec((kr, tn), lambda i, j: (1, j)),
            pl.BlockSpec((tm, tn), lambda i, j: (i, j)),
        ],
        out_specs=pl.BlockSpec((tm, tn), lambda i, j: (i, j)),
        out_shape=jax.ShapeDtypeStruct((m, n), F32),
        compiler_params=_params("arbitrary", "arbitrary"),
        name="out_proj",
    )(attn, rec, w_out, w_out, x)


def _down_norm_kernel(u_ref, w_ref, x_ref, g_ref, o_ref):
    k = pl.program_id(1)

    @pl.when(k == 0)
    def _():
        o_ref[...] = x_ref[...]

    o_ref[...] = jnp.dot(u_ref[...], w_ref[...], preferred_element_type=F32) + o_ref[...]

    @pl.when(k == pl.num_programs(1) - 1)
    def _():
        o_ref[...] = _rmsnorm_rows(o_ref[...], g_ref[...])


def _down_norm(u, w, x, g, *, tm, tk):
    m, kk = u.shape
    n = w.shape[1]
    return pl.pallas_call(
        _down_norm_kernel,
        grid=(m // tm, kk // tk),
        in_specs=[
            pl.BlockSpec((tm, tk), lambda i, k: (i, k)),
            pl.BlockSpec((tk, n), lambda i, k: (k, 0)),
            pl.BlockSpec((tm, n), lambda i, k: (i, 0), pipeline_mode=pl.Buffered(1)),
            pl.BlockSpec((1, n), lambda i, k: (0, 0)),
        ],
        out_specs=pl.BlockSpec((tm, n), lambda i, k: (i, 0)),
        out_shape=jax.ShapeDtypeStruct((m, n), F32),
        compiler_params=_params("arbitrary", "arbitrary"),
        name="down_norm",
    )(u, w, x, g)


def kernel(x, norm_mix_g, w_in, conv_w, conv_b, w_rgate, b_rgate, w_igate, b_igate, lru_lambda, lambda_q1, lambda_k1, lambda_q2, lambda_k2, subln_g, w_out, norm_mlp_g, w_mlp_up, w_mlp_down, norm_final_g):
    batch, seq, d_model = x.shape
    depth = w_in.shape[0]
    d_attn = ATTN_HEADS * HEAD_V_DIM
    d_lru = LRU_BLOCKS * LRU_BLOCK_DIM
    m = batch * seq
    row = lambda v: v.reshape(1, -1)

    xf = x.reshape(m, d_model)
    for l in range(depth):
        lambda_init = 0.8 - 0.6 * math.exp(-0.3 * l)
        w_in_l = w_in[l].astype(BF16)
        g_mix = row(norm_mix_g[l])
        qkv, h_mix = _norm_matmul(xf, g_mix, w_in_l, out_dtype=BF16, relu2=False,
                                  n_cols=3 * d_attn, scaled_cols=d_attn, scale=ATTN_Q_PRESCALE,
                                  return_normed=True, **IN_PROJ_TILES)

        lam = _diff_lambda(row(lambda_q1[l]), row(lambda_k1[l]), row(lambda_q2[l]),
                           row(lambda_k2[l]), lambda_init)
        attn, w_out_l, w_up_l, w_down_l = _diff_attention(
            qkv.reshape(batch, seq, 3 * d_attn), lam, subln_g[l].reshape(HEAD_V_DIM, 1),
            (w_out[l], w_mlp_up[l], w_mlp_down[l]),
            batch=batch, seq=seq, out_scale=1.0 - lambda_init)

        rec = _rec_branch(
            h_mix, w_in_l, 3 * d_attn, 3 * d_attn + d_lru, conv_w[l], row(conv_b[l]),
            w_rgate[l].astype(BF16), b_rgate[l].reshape(LRU_BLOCKS, 1, LRU_BLOCK_DIM),
            w_igate[l].astype(BF16), b_igate[l].reshape(LRU_BLOCKS, 1, LRU_BLOCK_DIM),
            row(lru_lambda[l]), seq=seq)

        xf = _out_proj(attn.reshape(m, d_attn), rec, w_out_l, xf, **OUT_PROJ_TILES)

        u = _norm_matmul(xf, row(norm_mlp_g[l]), w_up_l, out_dtype=BF16, relu2=True,
                         **UP_PROJ_TILES)
        last = l == depth - 1
        assert last, "DEPTH > 1 needs an un-normalised down-projection variant"
        xf = _down_norm(u, w_down_l, xf, row(norm_final_g), **DOWN_PROJ_TILES)
    return xf.reshape(batch, seq, d_model)
```

```python
import functools
import math

import jax
import jax.numpy as jnp
from jax import lax
from jax.experimental import pallas as pl
from jax.experimental.pallas import tpu as pltpu

F32 = jnp.float32
BF16 = jnp.bfloat16

EPS = 1e-6
CHUNK = 64
ATTN_HEADS = 16
HEAD_V_DIM = 128
HEAD_QK_DIM = 64
LRU_BLOCKS = 16
LRU_BLOCK_DIM = 128
CONV_WIDTH = 4
LRU_C = 8.0

V7X_VMEM_LIMIT_BYTES = 56 * 1024 * 1024
SUBLANES = 8
BF16_ROW_TILE = 16

IN_PROJ_TILES = dict(tm=512, tn=1024)
OUT_PROJ_TILES = dict(tm=1024, tn=512)
UP_PROJ_TILES = dict(tm=512, tn=1024)
DOWN_PROJ_TILES = dict(tm=512, tk=1024)
ATTN_TILE = 512
ATTN_HEADS_PER_STEP = 4
LRU_TILE = 512
LRU_CHANNEL_TILE = 256
LRU_ROW_SLAB = 32


def _params(*semantics):
    return pltpu.CompilerParams(dimension_semantics=semantics,
                                vmem_limit_bytes=V7X_VMEM_LIMIT_BYTES)


def _rmsnorm_rows(x, g):
    ms = jnp.mean(x * x, axis=-1, keepdims=True)
    return (x * lax.rsqrt(ms + EPS)) * g


def _norm_matmul_kernel(x_ref, g_ref, w_ref, o_ref, h_ref, *, relu2, scaled_tiles, scale):
    @pl.when(pl.program_id(1) == 0)
    def _():
        h_ref[...] = _rmsnorm_rows(x_ref[...], g_ref[...]).astype(BF16)

    acc = jnp.dot(h_ref[...], w_ref[...], preferred_element_type=F32)
    if relu2:
        acc = jnp.square(jnp.maximum(acc, 0.0))
    if scaled_tiles:
        acc = acc * jnp.where(pl.program_id(1) < scaled_tiles, scale, 1.0).astype(F32)
    o_ref[...] = acc.astype(o_ref.dtype)


def _norm_matmul(x, g, w, *, out_dtype, relu2, tm, tn, n_cols=None, scaled_cols=0, scale=1.0,
                 return_normed=False):
    m, k = x.shape
    n = w.shape[1] if n_cols is None else n_cols
    assert scaled_cols % tn == 0 and n % tn == 0
    out_specs = [pl.BlockSpec((tm, tn), lambda i, j: (i, j))]
    out_shape = [jax.ShapeDtypeStruct((m, n), out_dtype)]
    scratch = [pltpu.VMEM((tm, k), BF16)]
    if return_normed:
        out_specs.append(pl.BlockSpec((tm, k), lambda i, j: (i, 0)))
        out_shape.append(jax.ShapeDtypeStruct((m, k), BF16))
        scratch = []
    out = pl.pallas_call(
        functools.partial(_norm_matmul_kernel, relu2=relu2, scaled_tiles=scaled_cols // tn,
                          scale=scale),
        grid=(m // tm, n // tn),
        in_specs=[
            pl.BlockSpec((tm, k), lambda i, j: (i, 0)),
            pl.BlockSpec((1, k), lambda i, j: (0, 0)),
            pl.BlockSpec((k, tn), lambda i, j: (0, j)),
        ],
        out_specs=out_specs,
        out_shape=out_shape,
        scratch_shapes=scratch,
        compiler_params=_params("arbitrary", "arbitrary"),
        name="norm_matmul",
    )(x, g, w)
    return out if return_normed else out[0]


def _lambda_kernel(q1_ref, k1_ref, q2_ref, k2_ref, o_ref, *, lambda_init):
    d1 = jnp.sum(q1_ref[...] * k1_ref[...], axis=-1, keepdims=True)
    d2 = jnp.sum(q2_ref[...] * k2_ref[...], axis=-1, keepdims=True)
    o_ref[...] = jnp.exp(d1) - jnp.exp(d2) + lambda_init


def _diff_lambda(q1, k1, q2, k2, lambda_init):
    return pl.pallas_call(
        functools.partial(_lambda_kernel, lambda_init=lambda_init),
        out_shape=jax.ShapeDtypeStruct((1, 1), F32),
        name="diff_lambda",
    )(q1, k1, q2, k2)


ATTN_ONES_ROWS = 16
ATTN_Q_PRESCALE = HEAD_QK_DIM ** -0.5 * math.log2(math.e)


def _attn_kernel(lam_ref, g_ref, q_ref, k_ref, v_ref, *refs, tq, n_tiles, heads, out_scale,
                 n_cast):
    cast_in, o_ref, cast_out = refs[:n_cast], refs[n_cast], refs[n_cast + 1:2 * n_cast + 1]
    vt_ref, qm_ref, acc_ref, m_ref, s_ref = refs[2 * n_cast + 1:]
    qi = pl.program_id(2)
    dv = HEAD_V_DIM
    head_lanes = lambda h: slice(h * dv, (h + 1) * dv)

    for w_ref, wb_ref in zip(cast_in, cast_out):
        wb_ref[...] = w_ref[...].astype(BF16)

    @pl.when(qi == 0)
    def _():
        for h in range(heads):
            for j in range(n_tiles):
                vt = v_ref[0, j * tq:(j + 1) * tq, head_lanes(h)].astype(F32).T
                vt_ref[h, j, 0:dv, :] = vt.astype(BF16)
                vt_ref[h, j, dv:dv + ATTN_ONES_ROWS, :] = jnp.ones((ATTN_ONES_ROWS, tq), BF16)

    lane = lax.broadcasted_iota(jnp.int32, (tq, dv), 1)
    for h in range(heads):
        q = q_ref[0, :, head_lanes(h)]
        zero = jnp.zeros_like(q)
        qm_ref[h, 0] = jnp.where(lane < HEAD_QK_DIM, q, zero)
        qm_ref[h, 1] = jnp.where(lane >= HEAD_QK_DIM, q, zero)
    acc_ref[...] = jnp.zeros_like(acc_ref)
    m_ref[...] = jnp.full(m_ref.shape, -jnp.inf, F32)
    nt = (((1,), (1,)), ((), ()))

    def scores_into(s_ref, j, h):
        k = k_ref[0, pl.ds(pl.multiple_of(j * tq, tq), tq), head_lanes(h)]
        for mi in range(2):
            s_ref[mi] = lax.dot_general(k, qm_ref[h, mi], nt, preferred_element_type=F32)

    def softmax_pv(s_ref, j, h, masked):
        vt = vt_ref[h, j]
        if masked:
            key_chunk = lax.broadcasted_iota(jnp.int32, (tq, tq), 0) // CHUNK
            qry_chunk = lax.broadcasted_iota(jnp.int32, (tq, tq), 1) // CHUNK
            visible = key_chunk <= qry_chunk
            read = lambda mi: jnp.where(visible, s_ref[mi], -jnp.inf)
        else:
            read = lambda mi: s_ref[mi]
        for mi in range(2):
            m_old = m_ref[h, mi, 0:1, :]
            m_new = jnp.maximum(m_old, jnp.max(read(mi), axis=0, keepdims=True))
            alpha = jnp.exp2(m_old - m_new)
            p = jnp.exp2(read(mi) - m_new).astype(BF16)
            pv = jnp.dot(vt, p, preferred_element_type=F32)
            acc_ref[h, mi] = alpha * acc_ref[h, mi] + pv
            m_ref[h, mi] = jnp.broadcast_to(m_new, m_ref.shape[2:])

    scores_into(s_ref.at[0], 0, 0)

    def one_tile(j, masked):
        for h in range(heads):
            if h + 1 < heads:
                scores_into(s_ref.at[h + 1], j, h + 1)
            elif not masked:
                scores_into(s_ref.at[0], j + 1, 0)
            softmax_pv(s_ref.at[h], j, h, masked)

    def two_tiles(t, carry):
        one_tile(2 * t, False)
        one_tile(2 * t + 1, False)
        return carry

    lax.fori_loop(0, lax.shift_right_logical(qi, 1), two_tiles, 0)

    @pl.when(lax.bitwise_and(qi, 1) == 1)
    def _():
        one_tile(qi - 1, False)

    one_tile(qi, True)

    lam = lam_ref[0, 0]
    for h in range(heads):
        o1 = acc_ref[h, 0, 0:dv, :] * (1.0 / acc_ref[h, 0, dv:dv + 1, :])
        o2 = acc_ref[h, 1, 0:dv, :] * (lam / acc_ref[h, 1, dv:dv + 1, :])
        a = o1 - o2
        ms = jnp.mean(a * a, axis=0, keepdims=True)
        y = ((a * lax.rsqrt(ms + EPS)) * g_ref[...]) * out_scale
        o_ref[0, :, head_lanes(h)] = y.T.astype(o_ref.dtype)


def _diff_attention(qkv, lam, subln_g_col, cast_weights=(), *, batch, seq, out_scale,
                    tq=ATTN_TILE, heads=ATTN_HEADS_PER_STEP):
    assert ATTN_HEADS % heads == 0
    groups = ATTN_HEADS // heads
    n_tiles = seq // tq
    acc_rows = HEAD_V_DIM + ATTN_ONES_ROWS
    width = heads * HEAD_V_DIM
    n_steps = batch * groups * n_tiles
    step = lambda b, g, i: (b * groups + g) * n_tiles + i
    cast_specs = []
    for w in cast_weights:
        assert w.shape[0] % (n_steps * BF16_ROW_TILE) == 0
        cast_specs.append(pl.BlockSpec((w.shape[0] // n_steps, w.shape[1]),
                                       lambda b, g, i: (step(b, g, i), 0)))
    return pl.pallas_call(
        functools.partial(_attn_kernel, tq=tq, n_tiles=n_tiles, heads=heads, out_scale=out_scale,
                          n_cast=len(cast_weights)),
        grid=(batch, groups, n_tiles),
        in_specs=[
            pl.BlockSpec(memory_space=pltpu.SMEM),
            pl.BlockSpec((HEAD_V_DIM, 1), lambda b, g, i: (0, 0)),
            pl.BlockSpec((1, tq, width), lambda b, g, i: (b, i, g)),
            pl.BlockSpec((1, seq, width), lambda b, g, i: (b, 0, groups + g)),
            pl.BlockSpec((1, seq, width), lambda b, g, i: (b, 0, 2 * groups + g)),
        ] + cast_specs,
        out_specs=[pl.BlockSpec((1, tq, width), lambda b, g, i: (b, i, g))] + cast_specs,
        out_shape=[jax.ShapeDtypeStruct((batch, seq, ATTN_HEADS * HEAD_V_DIM), BF16)]
        + [jax.ShapeDtypeStruct(w.shape, BF16) for w in cast_weights],
        scratch_shapes=[pltpu.VMEM((heads, n_tiles, acc_rows, tq), BF16),
                        pltpu.VMEM((heads, 2, tq, HEAD_V_DIM), BF16),
                        pltpu.VMEM((heads, 2, acc_rows, tq), F32),
                        pltpu.VMEM((heads, 2, SUBLANES, tq), F32),
                        pltpu.VMEM((heads, 2, tq, tq), F32)],
        compiler_params=_params("arbitrary", "arbitrary", "arbitrary"),
        name="diff_attention",
    )(lam, subln_g_col, qkv, qkv, qkv, *cast_weights)


def _causal_conv(x, tail, cw, cb):
    ts = x.shape[0]
    xp = jnp.concatenate([tail, x], axis=0)
    xc = cb
    for j in range(CONV_WIDTH):
        off = SUBLANES - (CONV_WIDTH - 1) + j
        xc = xc + cw[j:j + 1, :] * xp[off:off + ts]
    return xc


def _block_gate_logits(xb, w_ref, b_ref):
    blocks = [jnp.dot(xb[:, n * LRU_BLOCK_DIM:(n + 1) * LRU_BLOCK_DIM], w_ref[n],
                      preferred_element_type=F32) + b_ref[n]
              for n in range(xb.shape[1] // LRU_BLOCK_DIM)]
    return jnp.concatenate(blocks, axis=1)


def _lru_scan(xc, r_logit, i_logit, gate, h0, log_sig):
    ts, c = xc.shape
    r = jax.nn.sigmoid(r_logit)
    i = jax.nn.sigmoid(i_logit)
    log_a = LRU_C * r * log_sig
    a = jnp.exp(log_a)
    b = jnp.sqrt(-jnp.tanh(log_a) * (1.0 + a * a)) * (i * xc)

    groups = ts // SUBLANES
    a = a.reshape(groups, SUBLANES, c)
    b = b.reshape(groups, SUBLANES, c)
    frame = lax.broadcasted_iota(jnp.int32, (1, SUBLANES, c), 1)
    d = 1
    while d < SUBLANES:
        keep = frame >= d
        b = b + a * jnp.where(keep, pltpu.roll(b, d, axis=1), 0.0)
        a = a * jnp.where(keep, pltpu.roll(a, d, axis=1), 1.0)
        d *= 2
    carry = h0
    hs = []
    for gi in range(groups):
        hg = b[gi] + a[gi] * carry
        carry = hg[SUBLANES - 1:SUBLANES, :]
        hs.append(hg)
    return jnp.concatenate(hs, axis=0) * jax.nn.gelu(gate, approximate=True), carry


def _rec_branch_kernel(h_ref, wx_ref, wg_ref, cw_ref, cb_ref, wr_ref, br_ref, wi_ref,
                       bi_ref, lam_ref, o_ref, xr_ref, gr_ref, xc_ref, rl_ref, il_ref,
                       tail_ref, state_ref, *, nj, tiles_per_seq):
    s = pl.program_id(0)
    prev = jnp.maximum(s - 1, 0)
    ip = prev // nj
    jp = lax.rem(prev, nj)
    ts = xr_ref.shape[0]

    @pl.when(s == 0)
    def _():
        xr_ref[...] = jnp.zeros_like(xr_ref)
        gr_ref[...] = jnp.zeros_like(gr_ref)
        tail_ref[...] = jnp.zeros_like(tail_ref)
        state_ref[...] = jnp.zeros_like(state_ref)

    seq_start = lax.rem(ip, tiles_per_seq) == 0
    tail = jnp.where(seq_start, 0.0, tail_ref[jp])
    carry = jnp.where(seq_start, 0.0, state_ref[jp, 0:1, :])
    cw, cb = cw_ref[...], cb_ref[...]
    slabs = [(r0, r0 + LRU_ROW_SLAB) for r0 in range(0, ts, LRU_ROW_SLAB)]

    for r0, r1 in slabs:
        before = tail if r0 == 0 else xr_ref[r0 - SUBLANES:r0, :]
        xc_ref[r0:r1, :] = _causal_conv(xr_ref[r0:r1, :], before, cw, cb)
    tail_ref[jp] = xr_ref[ts - SUBLANES:ts, :]

    xr_ref[...] = jnp.dot(h_ref[...], wx_ref[...], preferred_element_type=F32)
    xb = xc_ref[...].astype(BF16)
    rl_ref[...] = _block_gate_logits(xb, wr_ref, br_ref)
    il_ref[...] = _block_gate_logits(xb, wi_ref, bi_ref)

    lam = lam_ref[...]
    log_sig = jnp.minimum(lam, 0.0) - jnp.log1p(jnp.exp(-jnp.abs(lam)))
    for r0, r1 in slabs:
        rec, carry = _lru_scan(xc_ref[r0:r1, :], rl_ref[r0:r1, :], il_ref[r0:r1, :],
                               gr_ref[r0:r1, :], carry, log_sig)
        o_ref[r0:r1, :] = rec.astype(o_ref.dtype)
    state_ref[jp] = jnp.broadcast_to(carry, state_ref.shape[1:])
    gr_ref[...] = jnp.dot(h_ref[...], wg_ref[...], preferred_element_type=F32)


def _rec_branch(h, w_in, col_x, col_g, conv_w, conv_b, w_r, b_r, w_i, b_i, lam, *, seq,
                tm=LRU_TILE, tc=LRU_CHANNEL_TILE):
    m, k = h.shape
    d_lru = LRU_BLOCKS * LRU_BLOCK_DIM
    assert seq % tm == 0 and d_lru % tc == 0 and col_x % tc == 0 and col_g % tc == 0
    nj = d_lru // tc
    n_tiles = (m // tm) * nj
    nblk = tc // LRU_BLOCK_DIM
    this_i = lambda s: jnp.minimum(s, n_tiles - 1) // nj
    this_j = lambda s: lax.rem(jnp.minimum(s, n_tiles - 1), nj)
    prev_i = lambda s: jnp.maximum(s - 1, 0) // nj
    prev_j = lambda s: lax.rem(jnp.maximum(s - 1, 0), nj)
    vec = lambda s: (0, prev_j(s))
    blk = lambda s: (prev_j(s), 0, 0)
    return pl.pallas_call(
        functools.partial(_rec_branch_kernel, nj=nj, tiles_per_seq=seq // tm),
        grid=(n_tiles + 1,),
        in_specs=[
            pl.BlockSpec((tm, k), lambda s: (this_i(s), 0)),
            pl.BlockSpec((k, tc), lambda s: (0, col_x // tc + this_j(s))),
            pl.BlockSpec((k, tc), lambda s: (0, col_g // tc + this_j(s))),
            pl.BlockSpec((CONV_WIDTH, tc), vec),
            pl.BlockSpec((1, tc), vec),
            pl.BlockSpec((nblk, LRU_BLOCK_DIM, LRU_BLOCK_DIM), blk),
            pl.BlockSpec((nblk, 1, LRU_BLOCK_DIM), blk),
            pl.BlockSpec((nblk, LRU_BLOCK_DIM, LRU_BLOCK_DIM), blk),
            pl.BlockSpec((nblk, 1, LRU_BLOCK_DIM), blk),
            pl.BlockSpec((1, tc), vec),
        ],
        out_specs=pl.BlockSpec((tm, tc), lambda s: (prev_i(s), prev_j(s))),
        out_shape=jax.ShapeDtypeStruct((m, d_lru), BF16),
        scratch_shapes=[pltpu.VMEM((tm, tc), F32), pltpu.VMEM((tm, tc), F32),
                        pltpu.VMEM((tm, tc), F32),
                        pltpu.VMEM((tm, tc), F32), pltpu.VMEM((tm, tc), F32),
                        pltpu.VMEM((nj, SUBLANES, tc), F32),
                        pltpu.VMEM((nj, SUBLANES, tc), F32)],
        compiler_params=_params("arbitrary"),
        name="rec_branch",
    )(h, w_in, w_in, conv_w, conv_b, w_r, b_r, w_i, b_i, lam)


def _out_proj_kernel(a_ref, r_ref, wa_ref, wr_ref, x_ref, o_ref):
    acc = jnp.dot(a_ref[...], wa_ref[...], preferred_element_type=F32) + x_ref[...]
    o_ref[...] = jnp.dot(r_ref[...], wr_ref[...], preferred_element_type=F32) + acc


def _out_proj(attn, rec, w_out, x, *, tm, tn):
    m, ka = attn.shape
    kr = rec.shape[1]
    n = w_out.shape[1]
    assert ka == kr
    return pl.pallas_call(
        _out_proj_kernel,
        grid=(m // tm, n // tn),
        in_specs=[
            pl.BlockSpec((tm, ka), lambda i, j: (i, 0)),
            pl.BlockSpec((tm, kr), lambda i, j: (i, 0)),
            pl.BlockSpec((ka, tn), lambda i, j: (0, j)),
            pl.BlockSpec((kr, tn), lambda i, j: (1, j)),
            pl.BlockSpec((tm, tn), lambda i, j: (i, j)),
        ],
        out_specs=pl.BlockSpec((tm, tn), lambda i, j: (i, j)),
        out_shape=jax.ShapeDtypeStruct((m, n), F32),
        compiler_params=_params("arbitrary", "arbitrary"),
        name="out_proj",
    )(attn, rec, w_out, w_out, x)


def _down_norm_kernel(u_ref, w_ref, x_ref, g_ref, o_ref):
    k = pl.program_id(1)

    @pl.when(k == 0)
    def _():
        o_ref[...] = x_ref[...]

    o_ref[...] = jnp.dot(u_ref[...], w_ref[...], preferred_element_type=F32) + o_ref[...]

    @pl.when(k == pl.num_programs(1) - 1)
    def _():
        o_ref[...] = _rmsnorm_rows(o_ref[...], g_ref[...])


def _down_norm(u, w, x, g, *, tm, tk):
    m, kk = u.shape
    n = w.shape[1]
    return pl.pallas_call(
        _down_norm_kernel,
        grid=(m // tm, kk // tk),
        in_specs=[
            pl.BlockSpec((tm, tk), lambda i, k: (i, k)),
            pl.BlockSpec((tk, n), lambda i, k: (k, 0)),
            pl.BlockSpec((tm, n), lambda i, k: (i, 0), pipeline_mode=pl.Buffered(1)),
            pl.BlockSpec((1, n), lambda i, k: (0, 0)),
        ],
        out_specs=pl.BlockSpec((tm, n), lambda i, k: (i, 0)),
        out_shape=jax.ShapeDtypeStruct((m, n), F32),
        compiler_params=_params("arbitrary", "arbitrary"),
        name="down_norm",
    )(u, w, x, g)


def kernel(x, norm_mix_g, w_in, conv_w, conv_b, w_rgate, b_rgate, w_igate, b_igate, lru_lambda, lambda_q1, lambda_k1, lambda_q2, lambda_k2, subln_g, w_out, norm_mlp_g, w_mlp_up, w_mlp_down, norm_final_g):
    batch, seq, d_model = x.shape
    depth = w_in.shape[0]
    d_attn = ATTN_HEADS * HEAD_V_DIM
    d_lru = LRU_BLOCKS * LRU_BLOCK_DIM
    m = batch * seq
    row = lambda v: v.reshape(1, -1)

    xf = x.reshape(m, d_model)
    for l in range(depth):
        lambda_init = 0.8 - 0.6 * math.exp(-0.3 * l)
        w_in_l = w_in[l].astype(BF16)
        g_mix = row(norm_mix_g[l])
        qkv, h_mix = _norm_matmul(xf, g_mix, w_in_l, out_dtype=BF16, relu2=False,
                                  n_cols=3 * d_attn, scaled_cols=d_attn, scale=ATTN_Q_PRESCALE,
                                  return_normed=True, **IN_PROJ_TILES)

        lam = _diff_lambda(row(lambda_q1[l]), row(lambda_k1[l]), row(lambda_q2[l]),
                           row(lambda_k2[l]), lambda_init)
        attn, w_out_l, w_up_l, w_down_l = _diff_attention(
            qkv.reshape(batch, seq, 3 * d_attn), lam, subln_g[l].reshape(HEAD_V_DIM, 1),
            (w_out[l], w_mlp_up[l], w_mlp_down[l]),
            batch=batch, seq=seq, out_scale=1.0 - lambda_init)

        rec = _rec_branch(
            h_mix, w_in_l, 3 * d_attn, 3 * d_attn + d_lru, conv_w[l], row(conv_b[l]),
            w_rgate[l].astype(BF16), b_rgate[l].reshape(LRU_BLOCKS, 1, LRU_BLOCK_DIM),
            w_igate[l].astype(BF16), b_igate[l].reshape(LRU_BLOCKS, 1, LRU_BLOCK_DIM),
            row(lru_lambda[l]), seq=seq)

        xf = _out_proj(attn.reshape(m, d_attn), rec, w_out_l, xf, **OUT_PROJ_TILES)

        u = _norm_matmul(xf, row(norm_mlp_g[l]), w_up_l, out_dtype=BF16, relu2=True,
                         **UP_PROJ_TILES)
        last = l == depth - 1
        assert last, "DEPTH > 1 needs an un-normalised down-projection variant"
        xf = _down_norm(u, w_down_l, xf, row(norm_final_g), **DOWN_PROJ_TILES)
    return xf.reshape(batch, seq, d_model)
```

```python
import functools
import math

import jax
import jax.numpy as jnp
from jax import lax
from jax.experimental import pallas as pl
from jax.experimental.pallas import tpu as pltpu

F32 = jnp.float32
BF16 = jnp.bfloat16

EPS = 1e-6
CHUNK = 64
ATTN_HEADS = 16
HEAD_V_DIM = 128
HEAD_QK_DIM = 64
LRU_BLOCKS = 16
LRU_BLOCK_DIM = 128
CONV_WIDTH = 4
LRU_C = 8.0

V7X_VMEM_LIMIT_BYTES = 56 * 1024 * 1024
SUBLANES = 8
BF16_ROW_TILE = 16

IN_PROJ_TILES = dict(tm=512, tn=1024)
OUT_PROJ_TILES = dict(tm=1024, tn=1024)
UP_PROJ_TILES = dict(tm=512, tn=1024)
DOWN_PROJ_TILES = dict(tm=512, tk=1024)
ATTN_TILE = 512
ATTN_HEADS_PER_STEP = 4
LRU_TILE = 512
LRU_CHANNEL_TILE = 256
LRU_ROW_SLAB = 32


def _params(*semantics):
    return pltpu.CompilerParams(dimension_semantics=semantics,
                                vmem_limit_bytes=V7X_VMEM_LIMIT_BYTES)


def _rmsnorm_rows(x, g):
    ms = jnp.mean(x * x, axis=-1, keepdims=True)
    return (x * lax.rsqrt(ms + EPS)) * g


def _norm_matmul_kernel(x_ref, g_ref, w_ref, o_ref, h_ref, *, relu2, scaled_tiles, scale):
    @pl.when(pl.program_id(1) == 0)
    def _():
        h_ref[...] = _rmsnorm_rows(x_ref[...], g_ref[...]).astype(BF16)

    acc = jnp.dot(h_ref[...], w_ref[...], preferred_element_type=F32)
    if relu2:
        acc = jnp.square(jnp.maximum(acc, 0.0))
    if scaled_tiles:
        acc = acc * jnp.where(pl.program_id(1) < scaled_tiles, scale, 1.0).astype(F32)
    o_ref[...] = acc.astype(o_ref.dtype)


def _norm_matmul(x, g, w, *, out_dtype, relu2, tm, tn, n_cols=None, scaled_cols=0, scale=1.0,
                 return_normed=False):
    m, k = x.shape
    n = w.shape[1] if n_cols is None else n_cols
    assert scaled_cols % tn == 0 and n % tn == 0
    out_specs = [pl.BlockSpec((tm, tn), lambda i, j: (i, j))]
    out_shape = [jax.ShapeDtypeStruct((m, n), out_dtype)]
    scratch = [pltpu.VMEM((tm, k), BF16)]
    if return_normed:
        out_specs.append(pl.BlockSpec((tm, k), lambda i, j: (i, 0)))
        out_shape.append(jax.ShapeDtypeStruct((m, k), BF16))
        scratch = []
    out = pl.pallas_call(
        functools.partial(_norm_matmul_kernel, relu2=relu2, scaled_tiles=scaled_cols // tn,
                          scale=scale),
        grid=(m // tm, n // tn),
        in_specs=[
            pl.BlockSpec((tm, k), lambda i, j: (i, 0)),
            pl.BlockSpec((1, k), lambda i, j: (0, 0)),
            pl.BlockSpec((k, tn), lambda i, j: (0, j)),
        ],
        out_specs=out_specs,
        out_shape=out_shape,
        scratch_shapes=scratch,
        compiler_params=_params("arbitrary", "arbitrary"),
        name="norm_matmul",
    )(x, g, w)
    return out if return_normed else out[0]


def _lambda_kernel(q1_ref, k1_ref, q2_ref, k2_ref, o_ref, *, lambda_init):
    d1 = jnp.sum(q1_ref[...] * k1_ref[...], axis=-1, keepdims=True)
    d2 = jnp.sum(q2_ref[...] * k2_ref[...], axis=-1, keepdims=True)
    o_ref[...] = jnp.exp(d1) - jnp.exp(d2) + lambda_init


def _diff_lambda(q1, k1, q2, k2, lambda_init):
    return pl.pallas_call(
        functools.partial(_lambda_kernel, lambda_init=lambda_init),
        out_shape=jax.ShapeDtypeStruct((1, 1), F32),
        name="diff_lambda",
    )(q1, k1, q2, k2)


ATTN_ONES_ROWS = 16
ATTN_Q_PRESCALE = HEAD_QK_DIM ** -0.5 * math.log2(math.e)


def _attn_kernel(lam_ref, g_ref, q_ref, k_ref, v_ref, *refs, tq, n_tiles, heads, out_scale,
                 n_cast):
    cast_in, o_ref, cast_out = refs[:n_cast], refs[n_cast], refs[n_cast + 1:2 * n_cast + 1]
    vt_ref, qm_ref, acc_ref, m_ref, s_ref = refs[2 * n_cast + 1:]
    qi = pl.program_id(2)
    dv = HEAD_V_DIM
    head_lanes = lambda h: slice(h * dv, (h + 1) * dv)

    for w_ref, wb_ref in zip(cast_in, cast_out):
        wb_ref[...] = w_ref[...].astype(BF16)

    @pl.when(qi == 0)
    def _():
        for h in range(heads):
            for j in range(n_tiles):
                vt = v_ref[0, j * tq:(j + 1) * tq, head_lanes(h)].astype(F32).T
                vt_ref[h, j, 0:dv, :] = vt.astype(BF16)
                vt_ref[h, j, dv:dv + ATTN_ONES_ROWS, :] = jnp.ones((ATTN_ONES_ROWS, tq), BF16)

    lane = lax.broadcasted_iota(jnp.int32, (tq, dv), 1)
    for h in range(heads):
        q = q_ref[0, :, head_lanes(h)]
        zero = jnp.zeros_like(q)
        qm_ref[h, 0] = jnp.where(lane < HEAD_QK_DIM, q, zero)
        qm_ref[h, 1] = jnp.where(lane >= HEAD_QK_DIM, q, zero)
    acc_ref[...] = jnp.zeros_like(acc_ref)
    m_ref[...] = jnp.full(m_ref.shape, -jnp.inf, F32)
    nt = (((1,), (1,)), ((), ()))

    def scores_into(s_ref, j, h):
        k = k_ref[0, pl.ds(pl.multiple_of(j * tq, tq), tq), head_lanes(h)]
        for mi in range(2):
            s_ref[mi] = lax.dot_general(k, qm_ref[h, mi], nt, preferred_element_type=F32)

    def softmax_pv(s_ref, j, h, masked):
        vt = vt_ref[h, j]
        if masked:
            key_chunk = lax.broadcasted_iota(jnp.int32, (tq, tq), 0) // CHUNK
            qry_chunk = lax.broadcasted_iota(jnp.int32, (tq, tq), 1) // CHUNK
            visible = key_chunk <= qry_chunk
            read = lambda mi: jnp.where(visible, s_ref[mi], -jnp.inf)
        else:
            read = lambda mi: s_ref[mi]
        for mi in range(2):
            m_old = m_ref[h, mi, 0:1, :]
            m_new = jnp.maximum(m_old, jnp.max(read(mi), axis=0, keepdims=True))
            alpha = jnp.exp2(m_old - m_new)
            p = jnp.exp2(read(mi) - m_new).astype(BF16)
            pv = jnp.dot(vt, p, preferred_element_type=F32)
            acc_ref[h, mi] = alpha * acc_ref[h, mi] + pv
            m_ref[h, mi] = jnp.broadcast_to(m_new, m_ref.shape[2:])

    scores_into(s_ref.at[0], 0, 0)

    def one_tile(j, masked):
        for h in range(heads):
            if h + 1 < heads:
                scores_into(s_ref.at[h + 1], j, h + 1)
            elif not masked:
                scores_into(s_ref.at[0], j + 1, 0)
            softmax_pv(s_ref.at[h], j, h, masked)

    def two_tiles(t, carry):
        one_tile(2 * t, False)
        one_tile(2 * t + 1, False)
        return carry

    lax.fori_loop(0, lax.shift_right_logical(qi, 1), two_tiles, 0)

    @pl.when(lax.bitwise_and(qi, 1) == 1)
    def _():
        one_tile(qi - 1, False)

    one_tile(qi, True)

    lam = lam_ref[0, 0]
    for h in range(heads):
        o1 = acc_ref[h, 0, 0:dv, :] * (1.0 / acc_ref[h, 0, dv:dv + 1, :])
        o2 = acc_ref[h, 1, 0:dv, :] * (lam / acc_ref[h, 1, dv:dv + 1, :])
        a = o1 - o2
        ms = jnp.mean(a * a, axis=0, keepdims=True)
        y = ((a * lax.rsqrt(ms + EPS)) * g_ref[...]) * out_scale
        o_ref[0, :, head_lanes(h)] = y.T.astype(o_ref.dtype)


def _diff_attention(qkv, lam, subln_g_col, cast_weights=(), *, batch, seq, out_scale,
                    tq=ATTN_TILE, heads=ATTN_HEADS_PER_STEP):
    assert ATTN_HEADS % heads == 0
    groups = ATTN_HEADS // heads
    n_tiles = seq // tq
    acc_rows = HEAD_V_DIM + ATTN_ONES_ROWS
    width = heads * HEAD_V_DIM
    n_steps = batch * groups * n_tiles
    step = lambda b, g, i: (b * groups + g) * n_tiles + i
    cast_in_specs, cast_out_specs, cast_shapes = [], [], []
    for w, col0, ncols in cast_weights:
        assert w.shape[0] % (n_steps * BF16_ROW_TILE) == 0 and col0 % ncols == 0
        rows = w.shape[0] // n_steps
        cast_in_specs.append(pl.BlockSpec(
            (rows, ncols), lambda b, g, i, cb=col0 // ncols: (step(b, g, i), cb)))
        cast_out_specs.append(pl.BlockSpec((rows, ncols), lambda b, g, i: (step(b, g, i), 0)))
        cast_shapes.append(jax.ShapeDtypeStruct((w.shape[0], ncols), BF16))
    return pl.pallas_call(
        functools.partial(_attn_kernel, tq=tq, n_tiles=n_tiles, heads=heads, out_scale=out_scale,
                          n_cast=len(cast_weights)),
        grid=(batch, groups, n_tiles),
        in_specs=[
            pl.BlockSpec(memory_space=pltpu.SMEM),
            pl.BlockSpec((HEAD_V_DIM, 1), lambda b, g, i: (0, 0)),
            pl.BlockSpec((1, tq, width), lambda b, g, i: (b, i, g)),
            pl.BlockSpec((1, seq, width), lambda b, g, i: (b, 0, groups + g)),
            pl.BlockSpec((1, seq, width), lambda b, g, i: (b, 0, 2 * groups + g)),
        ] + cast_in_specs,
        out_specs=[pl.BlockSpec((1, tq, width), lambda b, g, i: (b, i, g))] + cast_out_specs,
        out_shape=[jax.ShapeDtypeStruct((batch, seq, ATTN_HEADS * HEAD_V_DIM), BF16)]
        + cast_shapes,
        scratch_shapes=[pltpu.VMEM((heads, n_tiles, acc_rows, tq), BF16),
                        pltpu.VMEM((heads, 2, tq, HEAD_V_DIM), BF16),
                        pltpu.VMEM((heads, 2, acc_rows, tq), F32),
                        pltpu.VMEM((heads, 2, SUBLANES, tq), F32),
                        pltpu.VMEM((heads, 2, tq, tq), F32)],
        compiler_params=_params("arbitrary", "arbitrary", "arbitrary"),
        name="diff_attention",
    )(lam, subln_g_col, qkv, qkv, qkv, *[w for w, _, _ in cast_weights])


def _causal_conv(x, tail, cw, cb):
    ts = x.shape[0]
    xp = jnp.concatenate([tail, x], axis=0)
    xc = cb
    for j in range(CONV_WIDTH):
        off = SUBLANES - (CONV_WIDTH - 1) + j
        xc = xc + cw[j:j + 1, :] * xp[off:off + ts]
    return xc


def _block_gate_logits(xb, w_ref, b_ref):
    blocks = [jnp.dot(xb[:, n * LRU_BLOCK_DIM:(n + 1) * LRU_BLOCK_DIM], w_ref[n],
                      preferred_element_type=F32) + b_ref[n]
              for n in range(xb.shape[1] // LRU_BLOCK_DIM)]
    return jnp.concatenate(blocks, axis=1)


def _lru_scan(xc, r_logit, i_logit, gate, h0, log_sig):
    ts, c = xc.shape
    r = jax.nn.sigmoid(r_logit)
    i = jax.nn.sigmoid(i_logit)
    log_a = LRU_C * r * log_sig
    a = jnp.exp(log_a)
    b = jnp.sqrt(-jnp.tanh(log_a) * (1.0 + a * a)) * (i * xc)

    groups = ts // SUBLANES
    a = a.reshape(groups, SUBLANES, c)
    b = b.reshape(groups, SUBLANES, c)
    frame = lax.broadcasted_iota(jnp.int32, (1, SUBLANES, c), 1)
    d = 1
    while d < SUBLANES:
        keep = frame >= d
        b = b + a * jnp.where(keep, pltpu.roll(b, d, axis=1), 0.0)
        a = a * jnp.where(keep, pltpu.roll(a, d, axis=1), 1.0)
        d *= 2
    carry = h0
    hs = []
    for gi in range(groups):
        hg = b[gi] + a[gi] * carry
        carry = hg[SUBLANES - 1:SUBLANES, :]
        hs.append(hg)
    return jnp.concatenate(hs, axis=0) * jax.nn.gelu(gate, approximate=True), carry


def _rec_branch_kernel(h_ref, wx_ref, wg_ref, cw_ref, cb_ref, wr_ref, br_ref, wi_ref,
                       bi_ref, lam_ref, o_ref, xr_ref, gr_ref, xc_ref, rl_ref, il_ref,
                       tail_ref, state_ref, *, nj, tiles_per_seq):
    s = pl.program_id(0)
    prev = jnp.maximum(s - 1, 0)
    ip = prev // nj
    jp = lax.rem(prev, nj)
    ts = xr_ref.shape[0]

    @pl.when(s == 0)
    def _():
        xr_ref[...] = jnp.zeros_like(xr_ref)
        gr_ref[...] = jnp.zeros_like(gr_ref)
        tail_ref[...] = jnp.zeros_like(tail_ref)
        state_ref[...] = jnp.zeros_like(state_ref)

    seq_start = lax.rem(ip, tiles_per_seq) == 0
    tail = jnp.where(seq_start, 0.0, tail_ref[jp])
    carry = jnp.where(seq_start, 0.0, state_ref[jp, 0:1, :])
    cw, cb = cw_ref[...], cb_ref[...]
    slabs = [(r0, r0 + LRU_ROW_SLAB) for r0 in range(0, ts, LRU_ROW_SLAB)]

    for r0, r1 in slabs:
        before = tail if r0 == 0 else xr_ref[r0 - SUBLANES:r0, :]
        xc_ref[r0:r1, :] = _causal_conv(xr_ref[r0:r1, :], before, cw, cb)
    tail_ref[jp] = xr_ref[ts - SUBLANES:ts, :]

    xr_ref[...] = jnp.dot(h_ref[...], wx_ref[...], preferred_element_type=F32)
    xb = xc_ref[...].astype(BF16)
    rl_ref[...] = _block_gate_logits(xb, wr_ref, br_ref)
    il_ref[...] = _block_gate_logits(xb, wi_ref, bi_ref)

    lam = lam_ref[...]
    log_sig = jnp.minimum(lam, 0.0) - jnp.log1p(jnp.exp(-jnp.abs(lam)))
    for r0, r1 in slabs:
        rec, carry = _lru_scan(xc_ref[r0:r1, :], rl_ref[r0:r1, :], il_ref[r0:r1, :],
                               gr_ref[r0:r1, :], carry, log_sig)
        o_ref[r0:r1, :] = rec.astype(o_ref.dtype)
    state_ref[jp] = jnp.broadcast_to(carry, state_ref.shape[1:])
    gr_ref[...] = jnp.dot(h_ref[...], wg_ref[...], preferred_element_type=F32)


def _rec_branch(h, w_x, w_g, conv_w, conv_b, w_r, b_r, w_i, b_i, lam, *, seq,
                tm=LRU_TILE, tc=LRU_CHANNEL_TILE):
    m, k = h.shape
    d_lru = LRU_BLOCKS * LRU_BLOCK_DIM
    assert seq % tm == 0 and d_lru % tc == 0 and w_x.shape == w_g.shape == (k, d_lru)
    nj = d_lru // tc
    n_tiles = (m // tm) * nj
    nblk = tc // LRU_BLOCK_DIM
    this_i = lambda s: jnp.minimum(s, n_tiles - 1) // nj
    this_j = lambda s: lax.rem(jnp.minimum(s, n_tiles - 1), nj)
    prev_i = lambda s: jnp.maximum(s - 1, 0) // nj
    prev_j = lambda s: lax.rem(jnp.maximum(s - 1, 0), nj)
    vec = lambda s: (0, prev_j(s))
    blk = lambda s: (prev_j(s), 0, 0)
    return pl.pallas_call(
        functools.partial(_rec_branch_kernel, nj=nj, tiles_per_seq=seq // tm),
        grid=(n_tiles + 1,),
        in_specs=[
            pl.BlockSpec((tm, k), lambda s: (this_i(s), 0)),
            pl.BlockSpec((k, tc), lambda s: (0, this_j(s))),
            pl.BlockSpec((k, tc), lambda s: (0, this_j(s))),
            pl.BlockSpec((CONV_WIDTH, tc), vec),
            pl.BlockSpec((1, tc), vec),
            pl.BlockSpec((nblk, LRU_BLOCK_DIM, LRU_BLOCK_DIM), blk),
            pl.BlockSpec((nblk, 1, LRU_BLOCK_DIM), blk),
            pl.BlockSpec((nblk, LRU_BLOCK_DIM, LRU_BLOCK_DIM), blk),
            pl.BlockSpec((nblk, 1, LRU_BLOCK_DIM), blk),
            pl.BlockSpec((1, tc), vec),
        ],
        out_specs=pl.BlockSpec((tm, tc), lambda s: (prev_i(s), prev_j(s))),
        out_shape=jax.ShapeDtypeStruct((m, d_lru), BF16),
        scratch_shapes=[pltpu.VMEM((tm, tc), F32), pltpu.VMEM((tm, tc), F32),
                        pltpu.VMEM((tm, tc), F32),
                        pltpu.VMEM((tm, tc), F32), pltpu.VMEM((tm, tc), F32),
                        pltpu.VMEM((nj, SUBLANES, tc), F32),
                        pltpu.VMEM((nj, SUBLANES, tc), F32)],
        compiler_params=_params("arbitrary"),
        name="rec_branch",
    )(h, w_x, w_g, conv_w, conv_b, w_r, b_r, w_i, b_i, lam)


def _out_proj_kernel(a_ref, r_ref, wa_ref, wr_ref, x_ref, o_ref):
    acc = jnp.dot(a_ref[...], wa_ref[...], preferred_element_type=F32) + x_ref[...]
    o_ref[...] = jnp.dot(r_ref[...], wr_ref[...], preferred_element_type=F32) + acc


def _out_proj(attn, rec, w_out, x, *, tm, tn):
    m, ka = attn.shape
    kr = rec.shape[1]
    n = w_out.shape[1]
    assert ka == kr
    return pl.pallas_call(
        _out_proj_kernel,
        grid=(m // tm, n // tn),
        in_specs=[
            pl.BlockSpec((tm, ka), lambda i, j: (i, 0)),
            pl.BlockSpec((tm, kr), lambda i, j: (i, 0)),
            pl.BlockSpec((ka, tn), lambda i, j: (0, j)),
            pl.BlockSpec((kr, tn), lambda i, j: (1, j)),
            pl.BlockSpec((tm, tn), lambda i, j: (i, j)),
        ],
        out_specs=pl.BlockSpec((tm, tn), lambda i, j: (i, j)),
        out_shape=jax.ShapeDtypeStruct((m, n), F32),
        compiler_params=_params("arbitrary", "arbitrary"),
        name="out_proj",
    )(attn, rec, w_out, w_out, x)


def _down_norm_kernel(u_ref, w_ref, x_ref, g_ref, o_ref):
    k = pl.program_id(1)

    @pl.when(k == 0)
    def _():
        o_ref[...] = x_ref[...]

    o_ref[...] = jnp.dot(u_ref[...], w_ref[...], preferred_element_type=F32) + o_ref[...]

    @pl.when(k == pl.num_programs(1) - 1)
    def _():
        o_ref[...] = _rmsnorm_rows(o_ref[...], g_ref[...])


def _down_norm(u, w, x, g, *, tm, tk):
    m, kk = u.shape
    n = w.shape[1]
    return pl.pallas_call(
        _down_norm_kernel,
        grid=(m // tm, kk // tk),
        in_specs=[
            pl.BlockSpec((tm, tk), lambda i, k: (i, k)),
            pl.BlockSpec((tk, n), lambda i, k: (k, 0)),
            pl.BlockSpec((tm, n), lambda i, k: (i, 0), pipeline_mode=pl.Buffered(1)),
            pl.BlockSpec((1, n), lambda i, k: (0, 0)),
        ],
        out_specs=pl.BlockSpec((tm, n), lambda i, k: (i, 0)),
        out_shape=jax.ShapeDtypeStruct((m, n), F32),
        compiler_params=_params("arbitrary", "arbitrary"),
        name="down_norm",
    )(u, w, x, g)


def kernel(x, norm_mix_g, w_in, conv_w, conv_b, w_rgate, b_rgate, w_igate, b_igate, lru_lambda, lambda_q1, lambda_k1, lambda_q2, lambda_k2, subln_g, w_out, norm_mlp_g, w_mlp_up, w_mlp_down, norm_final_g):
    batch, seq, d_model = x.shape
    depth = w_in.shape[0]
    d_attn = ATTN_HEADS * HEAD_V_DIM
    d_lru = LRU_BLOCKS * LRU_BLOCK_DIM
    m = batch * seq
    row = lambda v: v.reshape(1, -1)

    xf = x.reshape(m, d_model)
    for l in range(depth):
        lambda_init = 0.8 - 0.6 * math.exp(-0.3 * l)
        g_mix = row(norm_mix_g[l])
        w_qkv = w_in[l][:, :3 * d_attn].astype(BF16)
        qkv, h_mix = _norm_matmul(xf, g_mix, w_qkv, out_dtype=BF16, relu2=False,
                                  scaled_cols=d_attn, scale=ATTN_Q_PRESCALE,
                                  return_normed=True, **IN_PROJ_TILES)

        lam = _diff_lambda(row(lambda_q1[l]), row(lambda_k1[l]), row(lambda_q2[l]),
                           row(lambda_k2[l]), lambda_init)
        full = lambda w: (w, 0, w.shape[1])
        attn, w_x_l, w_g_l, w_out_l, w_up_l, w_down_l = _diff_attention(
            qkv.reshape(batch, seq, 3 * d_attn), lam, subln_g[l].reshape(HEAD_V_DIM, 1),
            ((w_in[l], 3 * d_attn, d_lru), (w_in[l], 3 * d_attn + d_lru, d_lru),
             full(w_out[l]), full(w_mlp_up[l]), full(w_mlp_down[l])),
            batch=batch, seq=seq, out_scale=1.0 - lambda_init)

        rec = _rec_branch(
            h_mix, w_x_l, w_g_l, conv_w[l], row(conv_b[l]),
            w_rgate[l].astype(BF16), b_rgate[l].reshape(LRU_BLOCKS, 1, LRU_BLOCK_DIM),
            w_igate[l].astype(BF16), b_igate[l].reshape(LRU_BLOCKS, 1, LRU_BLOCK_DIM),
            row(lru_lambda[l]), seq=seq)

        xf = _out_proj(attn.reshape(m, d_attn), rec, w_out_l, xf, **OUT_PROJ_TILES)

        u = _norm_matmul(xf, row(norm_mlp_g[l]), w_up_l, out_dtype=BF16, relu2=True,
                         **UP_PROJ_TILES)
        last = l == depth - 1
        assert last, "DEPTH > 1 needs an un-normalised down-projection variant"
        xf = _down_norm(u, w_down_l, xf, row(norm_final_g), **DOWN_PROJ_TILES)
    return xf.reshape(batch, seq, d_model)
```

```python
import functools
import math

import jax
import jax.numpy as jnp
from jax import lax
from jax.experimental import pallas as pl
from jax.experimental.pallas import tpu as pltpu

F32 = jnp.float32
BF16 = jnp.bfloat16

EPS = 1e-6
CHUNK = 64
ATTN_HEADS = 16
HEAD_V_DIM = 128
HEAD_QK_DIM = 64
LRU_BLOCKS = 16
LRU_BLOCK_DIM = 128
CONV_WIDTH = 4
LRU_C = 8.0

V7X_VMEM_LIMIT_BYTES = 56 * 1024 * 1024
SUBLANES = 8
BF16_ROW_TILE = 16

IN_PROJ_TILES = dict(tm=512, tn=1024)
OUT_PROJ_TILES = dict(tm=1024, tn=512)
UP_PROJ_TILES = dict(tm=1024, tn=1024)
DOWN_PROJ_TILES = dict(tm=512, tk=1024)
ATTN_TILE = 512
ATTN_HEADS_PER_STEP = 4
LRU_TILE = 512
LRU_CHANNEL_TILE = 512
LRU_ROW_SLAB = 32


def _params(*semantics):
    return pltpu.CompilerParams(dimension_semantics=semantics,
                                vmem_limit_bytes=V7X_VMEM_LIMIT_BYTES)


def _rmsnorm_rows(x, g):
    ms = jnp.mean(x * x, axis=-1, keepdims=True)
    return (x * lax.rsqrt(ms + EPS)) * g


def _norm_matmul_kernel(x_ref, g_ref, w_ref, o_ref, h_ref, *, relu2, scaled_tiles, scale):
    @pl.when(pl.program_id(1) == 0)
    def _():
        h_ref[...] = _rmsnorm_rows(x_ref[...], g_ref[...]).astype(BF16)

    acc = jnp.dot(h_ref[...], w_ref[...], preferred_element_type=F32)
    if relu2:
        acc = jnp.square(jnp.maximum(acc, 0.0))
    if scaled_tiles:
        acc = acc * jnp.where(pl.program_id(1) < scaled_tiles, scale, 1.0).astype(F32)
    o_ref[...] = acc.astype(o_ref.dtype)


def _norm_matmul(x, g, w, *, out_dtype, relu2, tm, tn, n_cols=None, scaled_cols=0, scale=1.0,
                 return_normed=False):
    m, k = x.shape
    n = w.shape[1] if n_cols is None else n_cols
    assert scaled_cols % tn == 0 and n % tn == 0
    out_specs = [pl.BlockSpec((tm, tn), lambda i, j: (i, j))]
    out_shape = [jax.ShapeDtypeStruct((m, n), out_dtype)]
    scratch = [pltpu.VMEM((tm, k), BF16)]
    if return_normed:
        out_specs.append(pl.BlockSpec((tm, k), lambda i, j: (i, 0)))
        out_shape.append(jax.ShapeDtypeStruct((m, k), BF16))
        scratch = []
    out = pl.pallas_call(
        functools.partial(_norm_matmul_kernel, relu2=relu2, scaled_tiles=scaled_cols // tn,
                          scale=scale),
        grid=(m // tm, n // tn),
        in_specs=[
            pl.BlockSpec((tm, k), lambda i, j: (i, 0)),
            pl.BlockSpec((1, k), lambda i, j: (0, 0)),
            pl.BlockSpec((k, tn), lambda i, j: (0, j)),
        ],
        out_specs=out_specs,
        out_shape=out_shape,
        scratch_shapes=scratch,
        compiler_params=_params("arbitrary", "arbitrary"),
        name="norm_matmul",
    )(x, g, w)
    return out if return_normed else out[0]


def _lambda_kernel(q1_ref, k1_ref, q2_ref, k2_ref, o_ref, *, lambda_init):
    d1 = jnp.sum(q1_ref[...] * k1_ref[...], axis=-1, keepdims=True)
    d2 = jnp.sum(q2_ref[...] * k2_ref[...], axis=-1, keepdims=True)
    o_ref[...] = jnp.exp(d1) - jnp.exp(d2) + lambda_init


def _diff_lambda(q1, k1, q2, k2, lambda_init):
    return pl.pallas_call(
        functools.partial(_lambda_kernel, lambda_init=lambda_init),
        out_shape=jax.ShapeDtypeStruct((1, 1), F32),
        name="diff_lambda",
    )(q1, k1, q2, k2)


ATTN_ONES_ROWS = 16
ATTN_Q_PRESCALE = HEAD_QK_DIM ** -0.5 * math.log2(math.e)


def _attn_kernel(lam_ref, g_ref, q_ref, k_ref, v_ref, *refs, tq, n_tiles, heads, out_scale,
                 n_cast):
    cast_in, o_ref, cast_out = refs[:n_cast], refs[n_cast], refs[n_cast + 1:2 * n_cast + 1]
    vt_ref, qm_ref, acc_ref, m_ref, s_ref = refs[2 * n_cast + 1:]
    qi = pl.program_id(2)
    dv = HEAD_V_DIM
    head_lanes = lambda h: slice(h * dv, (h + 1) * dv)

    for w_ref, wb_ref in zip(cast_in, cast_out):
        wb_ref[...] = w_ref[...].astype(BF16)

    @pl.when(qi == 0)
    def _():
        for h in range(heads):
            for j in range(n_tiles):
                vt = v_ref[0, j * tq:(j + 1) * tq, head_lanes(h)].astype(F32).T
                vt_ref[h, j, 0:dv, :] = vt.astype(BF16)
                vt_ref[h, j, dv:dv + ATTN_ONES_ROWS, :] = jnp.ones((ATTN_ONES_ROWS, tq), BF16)

    lane = lax.broadcasted_iota(jnp.int32, (tq, dv), 1)
    for h in range(heads):
        q = q_ref[0, :, head_lanes(h)]
        zero = jnp.zeros_like(q)
        qm_ref[h, 0] = jnp.where(lane < HEAD_QK_DIM, q, zero)
        qm_ref[h, 1] = jnp.where(lane >= HEAD_QK_DIM, q, zero)
    acc_ref[...] = jnp.zeros_like(acc_ref)
    m_ref[...] = jnp.full(m_ref.shape, -jnp.inf, F32)
    nt = (((1,), (1,)), ((), ()))

    def scores_into(s_ref, j, h):
        k = k_ref[0, pl.ds(pl.multiple_of(j * tq, tq), tq), head_lanes(h)]
        for mi in range(2):
            s_ref[mi] = lax.dot_general(k, qm_ref[h, mi], nt, preferred_element_type=F32)

    def softmax_pv(s_ref, j, h, masked):
        vt = vt_ref[h, j]
        if masked:
            key_chunk = lax.broadcasted_iota(jnp.int32, (tq, tq), 0) // CHUNK
            qry_chunk = lax.broadcasted_iota(jnp.int32, (tq, tq), 1) // CHUNK
            visible = key_chunk <= qry_chunk
            read = lambda mi: jnp.where(visible, s_ref[mi], -jnp.inf)
        else:
            read = lambda mi: s_ref[mi]
        for mi in range(2):
            m_old = m_ref[h, mi, 0:1, :]
            m_new = jnp.maximum(m_old, jnp.max(read(mi), axis=0, keepdims=True))
            alpha = jnp.exp2(m_old - m_new)
            p = jnp.exp2(read(mi) - m_new).astype(BF16)
            pv = jnp.dot(vt, p, preferred_element_type=F32)
            acc_ref[h, mi] = alpha * acc_ref[h, mi] + pv
            m_ref[h, mi] = jnp.broadcast_to(m_new, m_ref.shape[2:])

    scores_into(s_ref.at[0], 0, 0)

    def one_tile(j, masked):
        for h in range(heads):
            if h + 1 < heads:
                scores_into(s_ref.at[h + 1], j, h + 1)
            elif not masked:
                scores_into(s_ref.at[0], j + 1, 0)
            softmax_pv(s_ref.at[h], j, h, masked)

    def two_tiles(t, carry):
        one_tile(2 * t, False)
        one_tile(2 * t + 1, False)
        return carry

    lax.fori_loop(0, lax.shift_right_logical(qi, 1), two_tiles, 0)

    @pl.when(lax.bitwise_and(qi, 1) == 1)
    def _():
        one_tile(qi - 1, False)

    one_tile(qi, True)

    lam = lam_ref[0, 0]
    for h in range(heads):
        o1 = acc_ref[h, 0, 0:dv, :] * (1.0 / acc_ref[h, 0, dv:dv + 1, :])
        o2 = acc_ref[h, 1, 0:dv, :] * (lam / acc_ref[h, 1, dv:dv + 1, :])
        a = o1 - o2
        ms = jnp.mean(a * a, axis=0, keepdims=True)
        y = ((a * lax.rsqrt(ms + EPS)) * g_ref[...]) * out_scale
        o_ref[0, :, head_lanes(h)] = y.T.astype(o_ref.dtype)


def _diff_attention(qkv, lam, subln_g_col, cast_weights=(), *, batch, seq, out_scale,
                    tq=ATTN_TILE, heads=ATTN_HEADS_PER_STEP):
    assert ATTN_HEADS % heads == 0
    groups = ATTN_HEADS // heads
    n_tiles = seq // tq
    acc_rows = HEAD_V_DIM + ATTN_ONES_ROWS
    width = heads * HEAD_V_DIM
    n_steps = batch * groups * n_tiles
    step = lambda b, g, i: (b * groups + g) * n_tiles + i
    cast_in_specs, cast_out_specs, cast_shapes = [], [], []
    for w, col0, ncols in cast_weights:
        assert w.shape[0] % (n_steps * BF16_ROW_TILE) == 0 and col0 % ncols == 0
        rows = w.shape[0] // n_steps
        cast_in_specs.append(pl.BlockSpec(
            (rows, ncols), lambda b, g, i, cb=col0 // ncols: (step(b, g, i), cb)))
        cast_out_specs.append(pl.BlockSpec((rows, ncols), lambda b, g, i: (step(b, g, i), 0)))
        cast_shapes.append(jax.ShapeDtypeStruct((w.shape[0], ncols), BF16))
    return pl.pallas_call(
        functools.partial(_attn_kernel, tq=tq, n_tiles=n_tiles, heads=heads, out_scale=out_scale,
                          n_cast=len(cast_weights)),
        grid=(batch, groups, n_tiles),
        in_specs=[
            pl.BlockSpec(memory_space=pltpu.SMEM),
            pl.BlockSpec((HEAD_V_DIM, 1), lambda b, g, i: (0, 0)),
            pl.BlockSpec((1, tq, width), lambda b, g, i: (b, i, g)),
            pl.BlockSpec((1, seq, width), lambda b, g, i: (b, 0, groups + g)),
            pl.BlockSpec((1, seq, width), lambda b, g, i: (b, 0, 2 * groups + g)),
        ] + cast_in_specs,
        out_specs=[pl.BlockSpec((1, tq, width), lambda b, g, i: (b, i, g))] + cast_out_specs,
        out_shape=[jax.ShapeDtypeStruct((batch, seq, ATTN_HEADS * HEAD_V_DIM), BF16)]
        + cast_shapes,
        scratch_shapes=[pltpu.VMEM((heads, n_tiles, acc_rows, tq), BF16),
                        pltpu.VMEM((heads, 2, tq, HEAD_V_DIM), BF16),
                        pltpu.VMEM((heads, 2, acc_rows, tq), F32),
                        pltpu.VMEM((heads, 2, SUBLANES, tq), F32),
                        pltpu.VMEM((heads, 2, tq, tq), F32)],
        compiler_params=_params("arbitrary", "arbitrary", "arbitrary"),
        name="diff_attention",
    )(lam, subln_g_col, qkv, qkv, qkv, *[w for w, _, _ in cast_weights])


def _causal_conv(x, tail, cw, cb):
    ts = x.shape[0]
    xp = jnp.concatenate([tail, x], axis=0)
    xc = cb
    for j in range(CONV_WIDTH):
        off = SUBLANES - (CONV_WIDTH - 1) + j
        xc = xc + cw[j:j + 1, :] * xp[off:off + ts]
    return xc


def _block_gate_logits(xb, w_ref, b_ref):
    blocks = [jnp.dot(xb[:, n * LRU_BLOCK_DIM:(n + 1) * LRU_BLOCK_DIM], w_ref[n],
                      preferred_element_type=F32) + b_ref[n]
              for n in range(xb.shape[1] // LRU_BLOCK_DIM)]
    return jnp.concatenate(blocks, axis=1)


def _lru_scan(xc, r_logit, i_logit, gate, h0, log_sig):
    ts, c = xc.shape
    r = jax.nn.sigmoid(r_logit)
    i = jax.nn.sigmoid(i_logit)
    log_a = LRU_C * r * log_sig
    a = jnp.exp(log_a)
    b = jnp.sqrt(-jnp.tanh(log_a) * (1.0 + a * a)) * (i * xc)

    groups = ts // SUBLANES
    a = a.reshape(groups, SUBLANES, c)
    b = b.reshape(groups, SUBLANES, c)
    frame = lax.broadcasted_iota(jnp.int32, (1, SUBLANES, c), 1)
    d = 1
    while d < SUBLANES:
        keep = frame >= d
        b = b + a * jnp.where(keep, pltpu.roll(b, d, axis=1), 0.0)
        a = a * jnp.where(keep, pltpu.roll(a, d, axis=1), 1.0)
        d *= 2
    carry = h0
    hs = []
    for gi in range(groups):
        hg = b[gi] + a[gi] * carry
        carry = hg[SUBLANES - 1:SUBLANES, :]
        hs.append(hg)
    return jnp.concatenate(hs, axis=0) * jax.nn.gelu(gate, approximate=True), carry


def _rec_branch_kernel(h_ref, wx_ref, wg_ref, cw_ref, cb_ref, wr_ref, br_ref, wi_ref,
                       bi_ref, lam_ref, o_ref, xr_ref, gr_ref, xc_ref, rl_ref, il_ref,
                       tail_ref, state_ref, *, nj, tiles_per_seq):
    s = pl.program_id(0)
    prev = jnp.maximum(s - 1, 0)
    ip = prev // nj
    jp = lax.rem(prev, nj)
    ts = xr_ref.shape[0]

    @pl.when(s == 0)
    def _():
        xr_ref[...] = jnp.zeros_like(xr_ref)
        gr_ref[...] = jnp.zeros_like(gr_ref)
        tail_ref[...] = jnp.zeros_like(tail_ref)
        state_ref[...] = jnp.zeros_like(state_ref)

    seq_start = lax.rem(ip, tiles_per_seq) == 0
    tail = jnp.where(seq_start, 0.0, tail_ref[jp])
    carry = jnp.where(seq_start, 0.0, state_ref[jp, 0:1, :])
    cw, cb = cw_ref[...], cb_ref[...]
    slabs = [(r0, r0 + LRU_ROW_SLAB) for r0 in range(0, ts, LRU_ROW_SLAB)]

    for r0, r1 in slabs:
        before = tail if r0 == 0 else xr_ref[r0 - SUBLANES:r0, :]
        xc_ref[r0:r1, :] = _causal_conv(xr_ref[r0:r1, :], before, cw, cb)
    tail_ref[jp] = xr_ref[ts - SUBLANES:ts, :]

    xr_ref[...] = jnp.dot(h_ref[...], wx_ref[...], preferred_element_type=F32)
    xb = xc_ref[...].astype(BF16)
    rl_ref[...] = _block_gate_logits(xb, wr_ref, br_ref)
    il_ref[...] = _block_gate_logits(xb, wi_ref, bi_ref)

    lam = lam_ref[...]
    log_sig = jnp.minimum(lam, 0.0) - jnp.log1p(jnp.exp(-jnp.abs(lam)))
    for r0, r1 in slabs:
        rec, carry = _lru_scan(xc_ref[r0:r1, :], rl_ref[r0:r1, :], il_ref[r0:r1, :],
                               gr_ref[r0:r1, :], carry, log_sig)
        o_ref[r0:r1, :] = rec.astype(o_ref.dtype)
    state_ref[jp] = jnp.broadcast_to(carry, state_ref.shape[1:])
    gr_ref[...] = jnp.dot(h_ref[...], wg_ref[...], preferred_element_type=F32)


def _rec_branch(h, w_x, w_g, conv_w, conv_b, w_r, b_r, w_i, b_i, lam, *, seq,
                tm=LRU_TILE, tc=LRU_CHANNEL_TILE):
    m, k = h.shape
    d_lru = LRU_BLOCKS * LRU_BLOCK_DIM
    assert seq % tm == 0 and d_lru % tc == 0 and w_x.shape == w_g.shape == (k, d_lru)
    nj = d_lru // tc
    n_tiles = (m // tm) * nj
    nblk = tc // LRU_BLOCK_DIM
    this_i = lambda s: jnp.minimum(s, n_tiles - 1) // nj
    this_j = lambda s: lax.rem(jnp.minimum(s, n_tiles - 1), nj)
    prev_i = lambda s: jnp.maximum(s - 1, 0) // nj
    prev_j = lambda s: lax.rem(jnp.maximum(s - 1, 0), nj)
    vec = lambda s: (0, prev_j(s))
    blk = lambda s: (prev_j(s), 0, 0)
    return pl.pallas_call(
        functools.partial(_rec_branch_kernel, nj=nj, tiles_per_seq=seq // tm),
        grid=(n_tiles + 1,),
        in_specs=[
            pl.BlockSpec((tm, k), lambda s: (this_i(s), 0)),
            pl.BlockSpec((k, tc), lambda s: (0, this_j(s))),
            pl.BlockSpec((k, tc), lambda s: (0, this_j(s))),
            pl.BlockSpec((CONV_WIDTH, tc), vec),
            pl.BlockSpec((1, tc), vec),
            pl.BlockSpec((nblk, LRU_BLOCK_DIM, LRU_BLOCK_DIM), blk),
            pl.BlockSpec((nblk, 1, LRU_BLOCK_DIM), blk),
            pl.BlockSpec((nblk, LRU_BLOCK_DIM, LRU_BLOCK_DIM), blk),
            pl.BlockSpec((nblk, 1, LRU_BLOCK_DIM), blk),
            pl.BlockSpec((1, tc), vec),
        ],
        out_specs=pl.BlockSpec((tm, tc), lambda s: (prev_i(s), prev_j(s))),
        out_shape=jax.ShapeDtypeStruct((m, d_lru), BF16),
        scratch_shapes=[pltpu.VMEM((tm, tc), F32), pltpu.VMEM((tm, tc), F32),
                        pltpu.VMEM((tm, tc), F32),
                        pltpu.VMEM((tm, tc), F32), pltpu.VMEM((tm, tc), F32),
                        pltpu.VMEM((nj, SUBLANES, tc), F32),
                        pltpu.VMEM((nj, SUBLANES, tc), F32)],
        compiler_params=_params("arbitrary"),
        name="rec_branch",
    )(h, w_x, w_g, conv_w, conv_b, w_r, b_r, w_i, b_i, lam)


def _out_proj_kernel(a_ref, r_ref, wa_ref, wr_ref, x_ref, g_ref, o_ref, xg_ref, rstd_ref, ss_ref,
                     *, n):
    j = pl.program_id(1)
    acc = jnp.dot(a_ref[...], wa_ref[...], preferred_element_type=F32) + x_ref[...]
    x1 = jnp.dot(r_ref[...], wr_ref[...], preferred_element_type=F32) + acc
    o_ref[...] = x1
    xg_ref[...] = (x1 * g_ref[...]).astype(xg_ref.dtype)
    ss = jnp.sum(x1 * x1, axis=-1, keepdims=True)

    @pl.when(j == 0)
    def _():
        ss_ref[...] = ss

    @pl.when(j > 0)
    def _():
        ss_ref[...] += ss

    @pl.when(j == pl.num_programs(1) - 1)
    def _():
        rstd_ref[...] = lax.rsqrt(ss_ref[...] * (1.0 / n) + EPS)


def _out_proj(attn, rec, w_out, x, g_next, *, tm, tn):
    m, ka = attn.shape
    kr = rec.shape[1]
    n = w_out.shape[1]
    assert ka == kr
    tile = lambda i, j: (i, j)
    return pl.pallas_call(
        functools.partial(_out_proj_kernel, n=n),
        grid=(m // tm, n // tn),
        in_specs=[
            pl.BlockSpec((tm, ka), lambda i, j: (i, 0)),
            pl.BlockSpec((tm, kr), lambda i, j: (i, 0)),
            pl.BlockSpec((ka, tn), lambda i, j: (0, j)),
            pl.BlockSpec((kr, tn), lambda i, j: (1, j)),
            pl.BlockSpec((tm, tn), tile),
            pl.BlockSpec((1, tn), lambda i, j: (0, j)),
        ],
        out_specs=[pl.BlockSpec((tm, tn), tile), pl.BlockSpec((tm, tn), tile),
                   pl.BlockSpec((tm, 1), lambda i, j: (i, 0))],
        out_shape=[jax.ShapeDtypeStruct((m, n), F32), jax.ShapeDtypeStruct((m, n), BF16),
                   jax.ShapeDtypeStruct((m, 1), F32)],
        scratch_shapes=[pltpu.VMEM((tm, 1), F32)],
        compiler_params=_params("arbitrary", "arbitrary"),
        name="out_proj",
    )(attn, rec, w_out, w_out, x, g_next)


def _rowscaled_matmul_kernel(xg_ref, rstd_ref, w_ref, o_ref):
    acc = jnp.dot(xg_ref[...], w_ref[...], preferred_element_type=F32) * rstd_ref[...]
    o_ref[...] = jnp.square(jnp.maximum(acc, 0.0)).astype(o_ref.dtype)


def _rowscaled_relu2_matmul(xg, rstd, w, *, out_dtype, tm, tn):
    m, k = xg.shape
    n = w.shape[1]
    return pl.pallas_call(
        _rowscaled_matmul_kernel,
        grid=(m // tm, n // tn),
        in_specs=[
            pl.BlockSpec((tm, k), lambda i, j: (i, 0)),
            pl.BlockSpec((tm, 1), lambda i, j: (i, 0)),
            pl.BlockSpec((k, tn), lambda i, j: (0, j)),
        ],
        out_specs=pl.BlockSpec((tm, tn), lambda i, j: (i, j)),
        out_shape=jax.ShapeDtypeStruct((m, n), out_dtype),
        compiler_params=_params("arbitrary", "arbitrary"),
        name="up_proj",
    )(xg, rstd, w)


def _down_norm_kernel(u_ref, w_ref, x_ref, g_ref, o_ref):
    k = pl.program_id(1)

    @pl.when(k == 0)
    def _():
        o_ref[...] = x_ref[...]

    o_ref[...] = jnp.dot(u_ref[...], w_ref[...], preferred_element_type=F32) + o_ref[...]

    @pl.when(k == pl.num_programs(1) - 1)
    def _():
        o_ref[...] = _rmsnorm_rows(o_ref[...], g_ref[...])


def _down_norm(u, w, x, g, *, tm, tk):
    m, kk = u.shape
    n = w.shape[1]
    return pl.pallas_call(
        _down_norm_kernel,
        grid=(m // tm, kk // tk),
        in_specs=[
            pl.BlockSpec((tm, tk), lambda i, k: (i, k)),
            pl.BlockSpec((tk, n), lambda i, k: (k, 0)),
            pl.BlockSpec((tm, n), lambda i, k: (i, 0), pipeline_mode=pl.Buffered(1)),
            pl.BlockSpec((1, n), lambda i, k: (0, 0)),
        ],
        out_specs=pl.BlockSpec((tm, n), lambda i, k: (i, 0)),
        out_shape=jax.ShapeDtypeStruct((m, n), F32),
        compiler_params=_params("arbitrary", "arbitrary"),
        name="down_norm",
    )(u, w, x, g)


def kernel(x, norm_mix_g, w_in, conv_w, conv_b, w_rgate, b_rgate, w_igate, b_igate, lru_lambda, lambda_q1, lambda_k1, lambda_q2, lambda_k2, subln_g, w_out, norm_mlp_g, w_mlp_up, w_mlp_down, norm_final_g):
    batch, seq, d_model = x.shape
    depth = w_in.shape[0]
    d_attn = ATTN_HEADS * HEAD_V_DIM
    d_lru = LRU_BLOCKS * LRU_BLOCK_DIM
    m = batch * seq
    row = lambda v: v.reshape(1, -1)

    xf = x.reshape(m, d_model)
    for l in range(depth):
        lambda_init = 0.8 - 0.6 * math.exp(-0.3 * l)
        g_mix = row(norm_mix_g[l])
        w_qkv = w_in[l][:, :3 * d_attn].astype(BF16)
        qkv, h_mix = _norm_matmul(xf, g_mix, w_qkv, out_dtype=BF16, relu2=False,
                                  scaled_cols=d_attn, scale=ATTN_Q_PRESCALE,
                                  return_normed=True, **IN_PROJ_TILES)

        lam = _diff_lambda(row(lambda_q1[l]), row(lambda_k1[l]), row(lambda_q2[l]),
                           row(lambda_k2[l]), lambda_init)
        full = lambda w: (w, 0, w.shape[1])
        attn, w_x_l, w_g_l, w_out_l, w_up_l, w_down_l = _diff_attention(
            qkv.reshape(batch, seq, 3 * d_attn), lam, subln_g[l].reshape(HEAD_V_DIM, 1),
            ((w_in[l], 3 * d_attn, d_lru), (w_in[l], 3 * d_attn + d_lru, d_lru),
             full(w_out[l]), full(w_mlp_up[l]), full(w_mlp_down[l])),
            batch=batch, seq=seq, out_scale=1.0 - lambda_init)

        rec = _rec_branch(
            h_mix, w_x_l, w_g_l, conv_w[l], row(conv_b[l]),
            w_rgate[l].astype(BF16), b_rgate[l].reshape(LRU_BLOCKS, 1, LRU_BLOCK_DIM),
            w_igate[l].astype(BF16), b_igate[l].reshape(LRU_BLOCKS, 1, LRU_BLOCK_DIM),
            row(lru_lambda[l]), seq=seq)

        xf, xg, rstd = _out_proj(attn.reshape(m, d_attn), rec, w_out_l, xf, row(norm_mlp_g[l]),
                                 **OUT_PROJ_TILES)

        u = _rowscaled_relu2_matmul(xg, rstd, w_up_l, out_dtype=BF16, **UP_PROJ_TILES)
        last = l == depth - 1
        assert last, "DEPTH > 1 needs an un-normalised down-projection variant"
        xf = _down_norm(u, w_down_l, xf, row(norm_final_g), **DOWN_PROJ_TILES)
    return xf.reshape(batch, seq, d_model)
```

```python
import functools
import math

import jax
import jax.numpy as jnp
from jax import lax
from jax.experimental import pallas as pl
from jax.experimental.pallas import tpu as pltpu

F32 = jnp.float32
BF16 = jnp.bfloat16

EPS = 1e-6
CHUNK = 64
ATTN_HEADS = 16
HEAD_V_DIM = 128
HEAD_QK_DIM = 64
LRU_BLOCKS = 16
LRU_BLOCK_DIM = 128
CONV_WIDTH = 4
LRU_C = 8.0

V7X_VMEM_LIMIT_BYTES = 56 * 1024 * 1024
SUBLANES = 8
BF16_ROW_TILE = 16

IN_PROJ_TILES = dict(tm=512, tn=1024)
OUT_PROJ_TILES = dict(tm=1024, tn=512)
UP_PROJ_TILES = dict(tm=1024, tn=1024)
DOWN_PROJ_TILES = dict(tm=512, tk=2048)
ATTN_TILE = 512
ATTN_HEADS_PER_STEP = 4
LRU_TILE = 512
LRU_CHANNEL_TILE = 256
LRU_ROW_SLAB = 32
NORM_SLAB_ROWS = 32


def _params(*semantics):
    return pltpu.CompilerParams(dimension_semantics=semantics,
                                vmem_limit_bytes=V7X_VMEM_LIMIT_BYTES)


def _rmsnorm_rows(x, g):
    ms = jnp.mean(x * x, axis=-1, keepdims=True)
    return (x * lax.rsqrt(ms + EPS)) * g


def _norm_matmul_kernel(x_ref, g_ref, w_ref, o_ref, h_ref, *, relu2, scaled_tiles, scale):
    @pl.when(pl.program_id(1) == 0)
    def _():
        h_ref[...] = _rmsnorm_rows(x_ref[...], g_ref[...]).astype(BF16)

    acc = jnp.dot(h_ref[...], w_ref[...], preferred_element_type=F32)
    if relu2:
        acc = jnp.square(jnp.maximum(acc, 0.0))
    if scaled_tiles:
        acc = acc * jnp.where(pl.program_id(1) < scaled_tiles, scale, 1.0).astype(F32)
    o_ref[...] = acc.astype(o_ref.dtype)


def _norm_matmul(x, g, w, *, out_dtype, relu2, tm, tn, n_cols=None, scaled_cols=0, scale=1.0,
                 return_normed=False):
    m, k = x.shape
    n = w.shape[1] if n_cols is None else n_cols
    assert scaled_cols % tn == 0 and n % tn == 0
    out_specs = [pl.BlockSpec((tm, tn), lambda i, j: (i, j))]
    out_shape = [jax.ShapeDtypeStruct((m, n), out_dtype)]
    scratch = [pltpu.VMEM((tm, k), BF16)]
    if return_normed:
        out_specs.append(pl.BlockSpec((tm, k), lambda i, j: (i, 0)))
        out_shape.append(jax.ShapeDtypeStruct((m, k), BF16))
        scratch = []
    out = pl.pallas_call(
        functools.partial(_norm_matmul_kernel, relu2=relu2, scaled_tiles=scaled_cols // tn,
                          scale=scale),
        grid=(m // tm, n // tn),
        in_specs=[
            pl.BlockSpec((tm, k), lambda i, j: (i, 0)),
            pl.BlockSpec((1, k), lambda i, j: (0, 0)),
            pl.BlockSpec((k, tn), lambda i, j: (0, j)),
        ],
        out_specs=out_specs,
        out_shape=out_shape,
        scratch_shapes=scratch,
        compiler_params=_params("arbitrary", "arbitrary"),
        name="norm_matmul",
    )(x, g, w)
    return out if return_normed else out[0]


def _lambda_kernel(q1_ref, k1_ref, q2_ref, k2_ref, o_ref, *, lambda_init):
    d1 = jnp.sum(q1_ref[...] * k1_ref[...], axis=-1, keepdims=True)
    d2 = jnp.sum(q2_ref[...] * k2_ref[...], axis=-1, keepdims=True)
    o_ref[...] = jnp.exp(d1) - jnp.exp(d2) + lambda_init


def _diff_lambda(q1, k1, q2, k2, lambda_init):
    return pl.pallas_call(
        functools.partial(_lambda_kernel, lambda_init=lambda_init),
        out_shape=jax.ShapeDtypeStruct((1, 1), F32),
        name="diff_lambda",
    )(q1, k1, q2, k2)


ATTN_ONES_ROWS = 16
ATTN_Q_PRESCALE = HEAD_QK_DIM ** -0.5 * math.log2(math.e)


def _attn_kernel(lam_ref, g_ref, q_ref, k_ref, v_ref, *refs, tq, n_tiles, heads, out_scale,
                 n_cast):
    cast_in, o_ref, cast_out = refs[:n_cast], refs[n_cast], refs[n_cast + 1:2 * n_cast + 1]
    vt_ref, qm_ref, acc_ref, m_ref, s_ref = refs[2 * n_cast + 1:]
    qi = pl.program_id(2)
    dv = HEAD_V_DIM
    head_lanes = lambda h: slice(h * dv, (h + 1) * dv)

    for w_ref, wb_ref in zip(cast_in, cast_out):
        wb_ref[...] = w_ref[...].astype(BF16)

    @pl.when(qi == 0)
    def _():
        for h in range(heads):
            for j in range(n_tiles):
                vt = v_ref[0, j * tq:(j + 1) * tq, head_lanes(h)].astype(F32).T
                vt_ref[h, j, 0:dv, :] = vt.astype(BF16)
                vt_ref[h, j, dv:dv + ATTN_ONES_ROWS, :] = jnp.ones((ATTN_ONES_ROWS, tq), BF16)

    lane = lax.broadcasted_iota(jnp.int32, (tq, dv), 1)
    for h in range(heads):
        q = q_ref[0, :, head_lanes(h)]
        zero = jnp.zeros_like(q)
        qm_ref[h, 0] = jnp.where(lane < HEAD_QK_DIM, q, zero)
        qm_ref[h, 1] = jnp.where(lane >= HEAD_QK_DIM, q, zero)
    acc_ref[...] = jnp.zeros_like(acc_ref)
    m_ref[...] = jnp.full(m_ref.shape, -jnp.inf, F32)
    nt = (((1,), (1,)), ((), ()))

    def scores_into(s_ref, j, h):
        k = k_ref[0, pl.ds(pl.multiple_of(j * tq, tq), tq), head_lanes(h)]
        for mi in range(2):
            s_ref[mi] = lax.dot_general(k, qm_ref[h, mi], nt, preferred_element_type=F32)

    def softmax_pv(s_ref, j, h, masked):
        vt = vt_ref[h, j]
        if masked:
            key_chunk = lax.broadcasted_iota(jnp.int32, (tq, tq), 0) // CHUNK
            qry_chunk = lax.broadcasted_iota(jnp.int32, (tq, tq), 1) // CHUNK
            visible = key_chunk <= qry_chunk
            read = lambda mi: jnp.where(visible, s_ref[mi], -jnp.inf)
        else:
            read = lambda mi: s_ref[mi]
        for mi in range(2):
            m_old = m_ref[h, mi, 0:1, :]
            m_new = jnp.maximum(m_old, jnp.max(read(mi), axis=0, keepdims=True))
            alpha = jnp.exp2(m_old - m_new)
            p = jnp.exp2(read(mi) - m_new).astype(BF16)
            pv = jnp.dot(vt, p, preferred_element_type=F32)
            acc_ref[h, mi] = alpha * acc_ref[h, mi] + pv
            m_ref[h, mi] = jnp.broadcast_to(m_new, m_ref.shape[2:])

    scores_into(s_ref.at[0], 0, 0)

    def one_tile(j, masked):
        for h in range(heads):
            if h + 1 < heads:
                scores_into(s_ref.at[h + 1], j, h + 1)
            elif not masked:
                scores_into(s_ref.at[0], j + 1, 0)
            softmax_pv(s_ref.at[h], j, h, masked)

    def two_tiles(t, carry):
        one_tile(2 * t, False)
        one_tile(2 * t + 1, False)
        return carry

    lax.fori_loop(0, lax.shift_right_logical(qi, 1), two_tiles, 0)

    @pl.when(lax.bitwise_and(qi, 1) == 1)
    def _():
        one_tile(qi - 1, False)

    one_tile(qi, True)

    lam = lam_ref[0, 0]
    for h in range(heads):
        o1 = acc_ref[h, 0, 0:dv, :] * (1.0 / acc_ref[h, 0, dv:dv + 1, :])
        o2 = acc_ref[h, 1, 0:dv, :] * (lam / acc_ref[h, 1, dv:dv + 1, :])
        a = o1 - o2
        ms = jnp.mean(a * a, axis=0, keepdims=True)
        y = ((a * lax.rsqrt(ms + EPS)) * g_ref[...]) * out_scale
        o_ref[0, :, head_lanes(h)] = y.T.astype(o_ref.dtype)


def _diff_attention(qkv, lam, subln_g_col, cast_weights=(), *, batch, seq, out_scale,
                    tq=ATTN_TILE, heads=ATTN_HEADS_PER_STEP):
    assert ATTN_HEADS % heads == 0
    groups = ATTN_HEADS // heads
    n_tiles = seq // tq
    acc_rows = HEAD_V_DIM + ATTN_ONES_ROWS
    width = heads * HEAD_V_DIM
    n_steps = batch * groups * n_tiles
    step = lambda b, g, i: (b * groups + g) * n_tiles + i
    cast_in_specs, cast_out_specs, cast_shapes = [], [], []
    for w, col0, ncols in cast_weights:
        assert w.shape[0] % (n_steps * BF16_ROW_TILE) == 0 and col0 % ncols == 0
        rows = w.shape[0] // n_steps
        cast_in_specs.append(pl.BlockSpec(
            (rows, ncols), lambda b, g, i, cb=col0 // ncols: (step(b, g, i), cb)))
        cast_out_specs.append(pl.BlockSpec((rows, ncols), lambda b, g, i: (step(b, g, i), 0)))
        cast_shapes.append(jax.ShapeDtypeStruct((w.shape[0], ncols), BF16))
    return pl.pallas_call(
        functools.partial(_attn_kernel, tq=tq, n_tiles=n_tiles, heads=heads, out_scale=out_scale,
                          n_cast=len(cast_weights)),
        grid=(batch, groups, n_tiles),
        in_specs=[
            pl.BlockSpec(memory_space=pltpu.SMEM),
            pl.BlockSpec((HEAD_V_DIM, 1), lambda b, g, i: (0, 0)),
            pl.BlockSpec((1, tq, width), lambda b, g, i: (b, i, g)),
            pl.BlockSpec((1, seq, width), lambda b, g, i: (b, 0, groups + g)),
            pl.BlockSpec((1, seq, width), lambda b, g, i: (b, 0, 2 * groups + g)),
        ] + cast_in_specs,
        out_specs=[pl.BlockSpec((1, tq, width), lambda b, g, i: (b, i, g))] + cast_out_specs,
        out_shape=[jax.ShapeDtypeStruct((batch, seq, ATTN_HEADS * HEAD_V_DIM), BF16)]
        + cast_shapes,
        scratch_shapes=[pltpu.VMEM((heads, n_tiles, acc_rows, tq), BF16),
                        pltpu.VMEM((heads, 2, tq, HEAD_V_DIM), BF16),
                        pltpu.VMEM((heads, 2, acc_rows, tq), F32),
                        pltpu.VMEM((heads, 2, SUBLANES, tq), F32),
                        pltpu.VMEM((heads, 2, tq, tq), F32)],
        compiler_params=_params("arbitrary", "arbitrary", "arbitrary"),
        name="diff_attention",
    )(lam, subln_g_col, qkv, qkv, qkv, *[w for w, _, _ in cast_weights])


def _causal_conv(x, tail, cw, cb):
    ts = x.shape[0]
    xp = jnp.concatenate([tail, x], axis=0)
    xc = cb
    for j in range(CONV_WIDTH):
        off = SUBLANES - (CONV_WIDTH - 1) + j
        xc = xc + cw[j:j + 1, :] * xp[off:off + ts]
    return xc


def _block_gate_logits(xb, w_ref, b_ref):
    blocks = [jnp.dot(xb[:, n * LRU_BLOCK_DIM:(n + 1) * LRU_BLOCK_DIM], w_ref[n],
                      preferred_element_type=F32) + b_ref[n]
              for n in range(xb.shape[1] // LRU_BLOCK_DIM)]
    return jnp.concatenate(blocks, axis=1)


def _lru_scan(xc, r_logit, i_logit, gate, h0, log_sig):
    ts, c = xc.shape
    r = jax.nn.sigmoid(r_logit)
    i = jax.nn.sigmoid(i_logit)
    log_a = LRU_C * r * log_sig
    a = jnp.exp(log_a)
    b = jnp.sqrt(-jnp.tanh(log_a) * (1.0 + a * a)) * (i * xc)

    groups = ts // SUBLANES
    a = a.reshape(groups, SUBLANES, c)
    b = b.reshape(groups, SUBLANES, c)
    frame = lax.broadcasted_iota(jnp.int32, (1, SUBLANES, c), 1)
    d = 1
    while d < SUBLANES:
        keep = frame >= d
        b = b + a * jnp.where(keep, pltpu.roll(b, d, axis=1), 0.0)
        a = a * jnp.where(keep, pltpu.roll(a, d, axis=1), 1.0)
        d *= 2
    carry = h0
    hs = []
    for gi in range(groups):
        hg = b[gi] + a[gi] * carry
        carry = hg[SUBLANES - 1:SUBLANES, :]
        hs.append(hg)
    return jnp.concatenate(hs, axis=0) * jax.nn.gelu(gate, approximate=True), carry


def _rec_branch_kernel(h_ref, wx_ref, wg_ref, cw_ref, cb_ref, wr_ref, br_ref, wi_ref,
                       bi_ref, lam_ref, o_ref, xr_ref, gr_ref, xc_ref, rl_ref, il_ref,
                       tail_ref, state_ref, *, nj, tiles_per_seq):
    s = pl.program_id(0)
    prev = jnp.maximum(s - 1, 0)
    ip = prev // nj
    jp = lax.rem(prev, nj)
    ts = xr_ref.shape[0]

    @pl.when(s == 0)
    def _():
        xr_ref[...] = jnp.zeros_like(xr_ref)
        gr_ref[...] = jnp.zeros_like(gr_ref)
        tail_ref[...] = jnp.zeros_like(tail_ref)
        state_ref[...] = jnp.zeros_like(state_ref)

    seq_start = lax.rem(ip, tiles_per_seq) == 0
    tail = jnp.where(seq_start, 0.0, tail_ref[jp])
    carry = jnp.where(seq_start, 0.0, state_ref[jp, 0:1, :])
    cw, cb = cw_ref[...], cb_ref[...]
    slabs = [(r0, r0 + LRU_ROW_SLAB) for r0 in range(0, ts, LRU_ROW_SLAB)]

    for r0, r1 in slabs:
        before = tail if r0 == 0 else xr_ref[r0 - SUBLANES:r0, :]
        xc_ref[r0:r1, :] = _causal_conv(xr_ref[r0:r1, :], before, cw, cb)
    tail_ref[jp] = xr_ref[ts - SUBLANES:ts, :]

    xr_ref[...] = jnp.dot(h_ref[...], wx_ref[...], preferred_element_type=F32)
    xb = xc_ref[...].astype(BF16)
    rl_ref[...] = _block_gate_logits(xb, wr_ref, br_ref)
    il_ref[...] = _block_gate_logits(xb, wi_ref, bi_ref)

    lam = lam_ref[...]
    log_sig = jnp.minimum(lam, 0.0) - jnp.log1p(jnp.exp(-jnp.abs(lam)))
    for r0, r1 in slabs:
        rec, carry = _lru_scan(xc_ref[r0:r1, :], rl_ref[r0:r1, :], il_ref[r0:r1, :],
                               gr_ref[r0:r1, :], carry, log_sig)
        o_ref[r0:r1, :] = rec.astype(o_ref.dtype)
    state_ref[jp] = jnp.broadcast_to(carry, state_ref.shape[1:])
    gr_ref[...] = jnp.dot(h_ref[...], wg_ref[...], preferred_element_type=F32)


def _rec_branch(h, w_x, w_g, conv_w, conv_b, w_r, b_r, w_i, b_i, lam, *, seq,
                tm=LRU_TILE, tc=LRU_CHANNEL_TILE):
    m, k = h.shape
    d_lru = LRU_BLOCKS * LRU_BLOCK_DIM
    assert seq % tm == 0 and d_lru % tc == 0 and w_x.shape == w_g.shape == (k, d_lru)
    nj = d_lru // tc
    n_tiles = (m // tm) * nj
    nblk = tc // LRU_BLOCK_DIM
    this_i = lambda s: jnp.minimum(s, n_tiles - 1) // nj
    this_j = lambda s: lax.rem(jnp.minimum(s, n_tiles - 1), nj)
    prev_i = lambda s: jnp.maximum(s - 1, 0) // nj
    prev_j = lambda s: lax.rem(jnp.maximum(s - 1, 0), nj)
    vec = lambda s: (0, prev_j(s))
    blk = lambda s: (prev_j(s), 0, 0)
    return pl.pallas_call(
        functools.partial(_rec_branch_kernel, nj=nj, tiles_per_seq=seq // tm),
        grid=(n_tiles + 1,),
        in_specs=[
            pl.BlockSpec((tm, k), lambda s: (this_i(s), 0)),
            pl.BlockSpec((k, tc), lambda s: (0, this_j(s))),
            pl.BlockSpec((k, tc), lambda s: (0, this_j(s))),
            pl.BlockSpec((CONV_WIDTH, tc), vec),
            pl.BlockSpec((1, tc), vec),
            pl.BlockSpec((nblk, LRU_BLOCK_DIM, LRU_BLOCK_DIM), blk),
            pl.BlockSpec((nblk, 1, LRU_BLOCK_DIM), blk),
            pl.BlockSpec((nblk, LRU_BLOCK_DIM, LRU_BLOCK_DIM), blk),
            pl.BlockSpec((nblk, 1, LRU_BLOCK_DIM), blk),
            pl.BlockSpec((1, tc), vec),
        ],
        out_specs=pl.BlockSpec((tm, tc), lambda s: (prev_i(s), prev_j(s))),
        out_shape=jax.ShapeDtypeStruct((m, d_lru), BF16),
        scratch_shapes=[pltpu.VMEM((tm, tc), F32), pltpu.VMEM((tm, tc), F32),
                        pltpu.VMEM((tm, tc), F32),
                        pltpu.VMEM((tm, tc), F32), pltpu.VMEM((tm, tc), F32),
                        pltpu.VMEM((nj, SUBLANES, tc), F32),
                        pltpu.VMEM((nj, SUBLANES, tc), F32)],
        compiler_params=_params("arbitrary"),
        name="rec_branch",
    )(h, w_x, w_g, conv_w, conv_b, w_r, b_r, w_i, b_i, lam)


def _out_proj_kernel(a_ref, r_ref, wa_ref, wr_ref, x_ref, g_ref, o_ref, xg_ref, rstd_ref, ss_ref,
                     *, n):
    j = pl.program_id(1)
    acc = jnp.dot(a_ref[...], wa_ref[...], preferred_element_type=F32) + x_ref[...]
    x1 = jnp.dot(r_ref[...], wr_ref[...], preferred_element_type=F32) + acc
    o_ref[...] = x1
    xg_ref[...] = (x1 * g_ref[...]).astype(xg_ref.dtype)
    ss = jnp.sum(x1 * x1, axis=-1, keepdims=True)

    @pl.when(j == 0)
    def _():
        ss_ref[...] = ss

    @pl.when(j > 0)
    def _():
        ss_ref[...] += ss

    @pl.when(j == pl.num_programs(1) - 1)
    def _():
        rstd_ref[...] = lax.rsqrt(ss_ref[...] * (1.0 / n) + EPS)


def _out_proj(attn, rec, w_out, x, g_next, *, tm, tn):
    m, ka = attn.shape
    kr = rec.shape[1]
    n = w_out.shape[1]
    assert ka == kr
    tile = lambda i, j: (i, j)
    return pl.pallas_call(
        functools.partial(_out_proj_kernel, n=n),
        grid=(m // tm, n // tn),
        in_specs=[
            pl.BlockSpec((tm, ka), lambda i, j: (i, 0)),
            pl.BlockSpec((tm, kr), lambda i, j: (i, 0)),
            pl.BlockSpec((ka, tn), lambda i, j: (0, j)),
            pl.BlockSpec((kr, tn), lambda i, j: (1, j)),
            pl.BlockSpec((tm, tn), tile),
            pl.BlockSpec((1, tn), lambda i, j: (0, j)),
        ],
        out_specs=[pl.BlockSpec((tm, tn), tile), pl.BlockSpec((tm, tn), tile),
                   pl.BlockSpec((tm, 1), lambda i, j: (i, 0))],
        out_shape=[jax.ShapeDtypeStruct((m, n), F32), jax.ShapeDtypeStruct((m, n), BF16),
                   jax.ShapeDtypeStruct((m, 1), F32)],
        scratch_shapes=[pltpu.VMEM((tm, 1), F32)],
        compiler_params=_params("arbitrary", "arbitrary"),
        name="out_proj",
    )(attn, rec, w_out, w_out, x, g_next)


def _rowscaled_matmul_kernel(xg_ref, rstd_ref, w_ref, o_ref):
    acc = jnp.dot(xg_ref[...], w_ref[...], preferred_element_type=F32) * rstd_ref[...]
    o_ref[...] = jnp.square(jnp.maximum(acc, 0.0)).astype(o_ref.dtype)


def _rowscaled_relu2_matmul(xg, rstd, w, *, out_dtype, tm, tn):
    m, k = xg.shape
    n = w.shape[1]
    return pl.pallas_call(
        _rowscaled_matmul_kernel,
        grid=(m // tm, n // tn),
        in_specs=[
            pl.BlockSpec((tm, k), lambda i, j: (i, 0)),
            pl.BlockSpec((tm, 1), lambda i, j: (i, 0)),
            pl.BlockSpec((k, tn), lambda i, j: (0, j)),
        ],
        out_specs=pl.BlockSpec((tm, tn), lambda i, j: (i, j)),
        out_shape=jax.ShapeDtypeStruct((m, n), out_dtype),
        compiler_params=_params("arbitrary", "arbitrary"),
        name="up_proj",
    )(xg, rstd, w)


def _down_norm_kernel(u_ref, w_ref, x_ref, g_ref, o_ref):
    k = pl.program_id(1)

    @pl.when(k == 0)
    def _():
        o_ref[...] = x_ref[...]

    o_ref[...] = jnp.dot(u_ref[...], w_ref[...], preferred_element_type=F32) + o_ref[...]

    @pl.when(k == pl.num_programs(1) - 1)
    def _():
        for r0 in range(0, o_ref.shape[0], NORM_SLAB_ROWS):
            rows = slice(r0, r0 + NORM_SLAB_ROWS)
            o_ref[rows, :] = _rmsnorm_rows(o_ref[rows, :], g_ref[...])


def _down_norm(u, w, x, g, *, tm, tk):
    m, kk = u.shape
    n = w.shape[1]
    return pl.pallas_call(
        _down_norm_kernel,
        grid=(m // tm, kk // tk),
        in_specs=[
            pl.BlockSpec((tm, tk), lambda i, k: (i, k)),
            pl.BlockSpec((tk, n), lambda i, k: (k, 0)),
            pl.BlockSpec((tm, n), lambda i, k: (i, 0), pipeline_mode=pl.Buffered(1)),
            pl.BlockSpec((1, n), lambda i, k: (0, 0)),
        ],
        out_specs=pl.BlockSpec((tm, n), lambda i, k: (i, 0), pipeline_mode=pl.Buffered(1)),
        out_shape=jax.ShapeDtypeStruct((m, n), F32),
        compiler_params=_params("arbitrary", "arbitrary"),
        name="down_norm",
    )(u, w, x, g)


def kernel(x, norm_mix_g, w_in, conv_w, conv_b, w_rgate, b_rgate, w_igate, b_igate, lru_lambda, lambda_q1, lambda_k1, lambda_q2, lambda_k2, subln_g, w_out, norm_mlp_g, w_mlp_up, w_mlp_down, norm_final_g):
    batch, seq, d_model = x.shape
    depth = w_in.shape[0]
    d_attn = ATTN_HEADS * HEAD_V_DIM
    d_lru = LRU_BLOCKS * LRU_BLOCK_DIM
    m = batch * seq
    row = lambda v: v.reshape(1, -1)

    xf = x.reshape(m, d_model)
    for l in range(depth):
        lambda_init = 0.8 - 0.6 * math.exp(-0.3 * l)
        g_mix = row(norm_mix_g[l])
        w_qkv = w_in[l][:, :3 * d_attn].astype(BF16)
        qkv, h_mix = _norm_matmul(xf, g_mix, w_qkv, out_dtype=BF16, relu2=False,
                                  scaled_cols=d_attn, scale=ATTN_Q_PRESCALE,
                                  return_normed=True, **IN_PROJ_TILES)

        lam = _diff_lambda(row(lambda_q1[l]), row(lambda_k1[l]), row(lambda_q2[l]),
                           row(lambda_k2[l]), lambda_init)
        full = lambda w: (w, 0, w.shape[1])
        attn, w_x_l, w_g_l, w_out_l, w_up_l, w_down_l = _diff_attention(
            qkv.reshape(batch, seq, 3 * d_attn), lam, subln_g[l].reshape(HEAD_V_DIM, 1),
            ((w_in[l], 3 * d_attn, d_lru), (w_in[l], 3 * d_attn + d_lru, d_lru),
             full(w_out[l]), full(w_mlp_up[l]), full(w_mlp_down[l])),
            batch=batch, seq=seq, out_scale=1.0 - lambda_init)

        rec = _rec_branch(
            h_mix, w_x_l, w_g_l, conv_w[l], row(conv_b[l]),
            w_rgate[l].astype(BF16), b_rgate[l].reshape(LRU_BLOCKS, 1, LRU_BLOCK_DIM),
            w_igate[l].astype(BF16), b_igate[l].reshape(LRU_BLOCKS, 1, LRU_BLOCK_DIM),
            row(lru_lambda[l]), seq=seq)

        xf, xg, rstd = _out_proj(attn.reshape(m, d_attn), rec, w_out_l, xf, row(norm_mlp_g[l]),
                                 **OUT_PROJ_TILES)

        u = _rowscaled_relu2_matmul(xg, rstd, w_up_l, out_dtype=BF16, **UP_PROJ_TILES)
        last = l == depth - 1
        assert last, "DEPTH > 1 needs an un-normalised down-projection variant"
        xf = _down_norm(u, w_down_l, xf, row(norm_final_g), **DOWN_PROJ_TILES)
    return xf.reshape(batch, seq, d_model)
```

```python
import functools
import math

import jax
import jax.numpy as jnp
from jax import lax
from jax.experimental import pallas as pl
from jax.experimental.pallas import tpu as pltpu

F32 = jnp.float32
BF16 = jnp.bfloat16

EPS = 1e-6
CHUNK = 64
ATTN_HEADS = 16
HEAD_V_DIM = 128
HEAD_QK_DIM = 64
LRU_BLOCKS = 16
LRU_BLOCK_DIM = 128
CONV_WIDTH = 4
LRU_C = 8.0

V7X_VMEM_LIMIT_BYTES = 56 * 1024 * 1024
SUBLANES = 8
BF16_ROW_TILE = 16

IN_PROJ_TILES = dict(tm=512, tn=1024)
OUT_PROJ_TILES = dict(tm=1024, tn=512)
UP_PROJ_TILES = dict(tm=1024, tn=1024)
DOWN_PROJ_TILES = dict(tm=512, tk=2048)
ATTN_TILE = 512
ATTN_HEADS_PER_STEP = 4
LRU_TILE = 512
LRU_CHANNEL_TILE = 256
LRU_ROW_SLAB = 32
NORM_SLAB_ROWS = 32


def _params(*semantics):
    return pltpu.CompilerParams(dimension_semantics=semantics,
                                vmem_limit_bytes=V7X_VMEM_LIMIT_BYTES)


def _rmsnorm_rows(x, g):
    ms = jnp.mean(x * x, axis=-1, keepdims=True)
    return (x * lax.rsqrt(ms + EPS)) * g


def _norm_matmul_kernel(x_ref, g_ref, w_ref, o_ref, h_ref, *, relu2, scaled_tiles, scale):
    @pl.when(pl.program_id(1) == 0)
    def _():
        h_ref[...] = _rmsnorm_rows(x_ref[...], g_ref[...]).astype(BF16)

    acc = jnp.dot(h_ref[...], w_ref[...], preferred_element_type=F32)
    if relu2:
        acc = jnp.square(jnp.maximum(acc, 0.0))
    if scaled_tiles:
        acc = acc * jnp.where(pl.program_id(1) < scaled_tiles, scale, 1.0).astype(F32)
    o_ref[...] = acc.astype(o_ref.dtype)


def _norm_matmul(x, g, w, *, out_dtype, relu2, tm, tn, n_cols=None, scaled_cols=0, scale=1.0,
                 return_normed=False):
    m, k = x.shape
    n = w.shape[1] if n_cols is None else n_cols
    assert scaled_cols % tn == 0 and n % tn == 0
    out_specs = [pl.BlockSpec((tm, tn), lambda i, j: (i, j))]
    out_shape = [jax.ShapeDtypeStruct((m, n), out_dtype)]
    scratch = [pltpu.VMEM((tm, k), BF16)]
    if return_normed:
        out_specs.append(pl.BlockSpec((tm, k), lambda i, j: (i, 0)))
        out_shape.append(jax.ShapeDtypeStruct((m, k), BF16))
        scratch = []
    out = pl.pallas_call(
        functools.partial(_norm_matmul_kernel, relu2=relu2, scaled_tiles=scaled_cols // tn,
                          scale=scale),
        grid=(m // tm, n // tn),
        in_specs=[
            pl.BlockSpec((tm, k), lambda i, j: (i, 0)),
            pl.BlockSpec((1, k), lambda i, j: (0, 0)),
            pl.BlockSpec((k, tn), lambda i, j: (0, j)),
        ],
        out_specs=out_specs,
        out_shape=out_shape,
        scratch_shapes=scratch,
        compiler_params=_params("arbitrary", "arbitrary"),
        name="norm_matmul",
    )(x, g, w)
    return out if return_normed else out[0]


def _lambda_kernel(q1_ref, k1_ref, q2_ref, k2_ref, o_ref, *, lambda_init):
    d1 = jnp.sum(q1_ref[...] * k1_ref[...], axis=-1, keepdims=True)
    d2 = jnp.sum(q2_ref[...] * k2_ref[...], axis=-1, keepdims=True)
    o_ref[...] = jnp.exp(d1) - jnp.exp(d2) + lambda_init


def _diff_lambda(q1, k1, q2, k2, lambda_init):
    return pl.pallas_call(
        functools.partial(_lambda_kernel, lambda_init=lambda_init),
        out_shape=jax.ShapeDtypeStruct((1, 1), F32),
        name="diff_lambda",
    )(q1, k1, q2, k2)


ATTN_ONES_ROWS = 16
ATTN_Q_PRESCALE = HEAD_QK_DIM ** -0.5 * math.log2(math.e)


def _attn_kernel(lam_ref, g_ref, q_ref, k_ref, v_ref, *refs, tq, n_tiles, heads, out_scale,
                 n_cast):
    cast_in, o_ref, cast_out = refs[:n_cast], refs[n_cast], refs[n_cast + 1:2 * n_cast + 1]
    vt_ref, qm_ref, acc_ref, m_ref, s_ref = refs[2 * n_cast + 1:]
    qi = pl.program_id(2)
    dv = HEAD_V_DIM
    head_lanes = lambda h: slice(h * dv, (h + 1) * dv)

    for w_ref, wb_ref in zip(cast_in, cast_out):
        wb_ref[...] = w_ref[...].astype(BF16)

    @pl.when(qi == 0)
    def _():
        for h in range(heads):
            for j in range(n_tiles):
                vt = v_ref[0, j * tq:(j + 1) * tq, head_lanes(h)].astype(F32).T
                vt_ref[h, j, 0:dv, :] = vt.astype(BF16)
                vt_ref[h, j, dv:dv + ATTN_ONES_ROWS, :] = jnp.ones((ATTN_ONES_ROWS, tq), BF16)

    lane = lax.broadcasted_iota(jnp.int32, (tq, dv), 1)
    for h in range(heads):
        q = q_ref[0, :, head_lanes(h)]
        zero = jnp.zeros_like(q)
        qm_ref[h, 0] = jnp.where(lane < HEAD_QK_DIM, q, zero)
        qm_ref[h, 1] = jnp.where(lane >= HEAD_QK_DIM, q, zero)
    acc_ref[...] = jnp.zeros_like(acc_ref)
    m_ref[...] = jnp.full(m_ref.shape, -jnp.inf, F32)
    nt = (((1,), (1,)), ((), ()))

    def scores_into(s_ref, j, h):
        k = k_ref[0, pl.ds(pl.multiple_of(j * tq, tq), tq), head_lanes(h)]
        for mi in range(2):
            s_ref[mi] = lax.dot_general(k, qm_ref[h, mi], nt, preferred_element_type=F32)

    def softmax_pv(s_ref, j, h, masked):
        vt = vt_ref[h, j]
        if masked:
            key_chunk = lax.broadcasted_iota(jnp.int32, (tq, tq), 0) // CHUNK
            qry_chunk = lax.broadcasted_iota(jnp.int32, (tq, tq), 1) // CHUNK
            visible = key_chunk <= qry_chunk
            read = lambda mi: jnp.where(visible, s_ref[mi], -jnp.inf)
        else:
            read = lambda mi: s_ref[mi]
        for mi in range(2):
            m_old = m_ref[h, mi, 0:1, :]
            m_new = jnp.maximum(m_old, jnp.max(read(mi), axis=0, keepdims=True))
            alpha = jnp.exp2(m_old - m_new)
            p = jnp.exp2(read(mi) - m_new).astype(BF16)
            pv = jnp.dot(vt, p, preferred_element_type=F32)
            acc_ref[h, mi] = alpha * acc_ref[h, mi] + pv
            m_ref[h, mi] = jnp.broadcast_to(m_new, m_ref.shape[2:])

    scores_into(s_ref.at[0], 0, 0)

    def one_tile(j, masked):
        for h in range(heads):
            if h + 1 < heads:
                scores_into(s_ref.at[h + 1], j, h + 1)
            elif not masked:
                scores_into(s_ref.at[0], j + 1, 0)
            softmax_pv(s_ref.at[h], j, h, masked)

    def two_tiles(t, carry):
        one_tile(2 * t, False)
        one_tile(2 * t + 1, False)
        return carry

    lax.fori_loop(0, lax.shift_right_logical(qi, 1), two_tiles, 0)

    @pl.when(lax.bitwise_and(qi, 1) == 1)
    def _():
        one_tile(qi - 1, False)

    one_tile(qi, True)

    lam = lam_ref[0, 0]
    for h in range(heads):
        o1 = acc_ref[h, 0, 0:dv, :] * (1.0 / acc_ref[h, 0, dv:dv + 1, :])
        o2 = acc_ref[h, 1, 0:dv, :] * (lam / acc_ref[h, 1, dv:dv + 1, :])
        a = o1 - o2
        ms = jnp.mean(a * a, axis=0, keepdims=True)
        y = ((a * lax.rsqrt(ms + EPS)) * g_ref[...]) * out_scale
        o_ref[0, :, head_lanes(h)] = y.T.astype(o_ref.dtype)


def _diff_attention(qkv, lam, subln_g_col, cast_weights=(), *, batch, seq, out_scale,
                    tq=ATTN_TILE, heads=ATTN_HEADS_PER_STEP):
    assert ATTN_HEADS % heads == 0
    groups = ATTN_HEADS // heads
    n_tiles = seq // tq
    acc_rows = HEAD_V_DIM + ATTN_ONES_ROWS
    width = heads * HEAD_V_DIM
    n_steps = batch * groups * n_tiles
    step = lambda b, g, i: (b * groups + g) * n_tiles + i
    cast_in_specs, cast_out_specs, cast_shapes = [], [], []
    for w, col0, ncols in cast_weights:
        assert w.shape[0] % (n_steps * BF16_ROW_TILE) == 0 and col0 % ncols == 0
        rows = w.shape[0] // n_steps
        cast_in_specs.append(pl.BlockSpec(
            (rows, ncols), lambda b, g, i, cb=col0 // ncols: (step(b, g, i), cb)))
        cast_out_specs.append(pl.BlockSpec((rows, ncols), lambda b, g, i: (step(b, g, i), 0)))
        cast_shapes.append(jax.ShapeDtypeStruct((w.shape[0], ncols), BF16))
    return pl.pallas_call(
        functools.partial(_attn_kernel, tq=tq, n_tiles=n_tiles, heads=heads, out_scale=out_scale,
                          n_cast=len(cast_weights)),
        grid=(batch, groups, n_tiles),
        in_specs=[
            pl.BlockSpec(memory_space=pltpu.SMEM),
            pl.BlockSpec((HEAD_V_DIM, 1), lambda b, g, i: (0, 0)),
            pl.BlockSpec((1, tq, width), lambda b, g, i: (b, i, g)),
            pl.BlockSpec((1, seq, width), lambda b, g, i: (b, 0, groups + g)),
            pl.BlockSpec((1, seq, width), lambda b, g, i: (b, 0, 2 * groups + g)),
        ] + cast_in_specs,
        out_specs=[pl.BlockSpec((1, tq, width), lambda b, g, i: (b, i, g))] + cast_out_specs,
        out_shape=[jax.ShapeDtypeStruct((batch, seq, ATTN_HEADS * HEAD_V_DIM), BF16)]
        + cast_shapes,
        scratch_shapes=[pltpu.VMEM((heads, n_tiles, acc_rows, tq), BF16),
                        pltpu.VMEM((heads, 2, tq, HEAD_V_DIM), BF16),
                        pltpu.VMEM((heads, 2, acc_rows, tq), F32),
                        pltpu.VMEM((heads, 2, SUBLANES, tq), F32),
                        pltpu.VMEM((heads, 2, tq, tq), F32)],
        compiler_params=_params("arbitrary", "arbitrary", "arbitrary"),
        name="diff_attention",
    )(lam, subln_g_col, qkv, qkv, qkv, *[w for w, _, _ in cast_weights])


def _causal_conv(x, tail, cw, cb):
    ts = x.shape[0]
    xp = jnp.concatenate([tail, x], axis=0)
    xc = cb
    for j in range(CONV_WIDTH):
        off = SUBLANES - (CONV_WIDTH - 1) + j
        xc = xc + cw[j:j + 1, :] * xp[off:off + ts]
    return xc


def _block_gate_logits(xb, w_ref, b_ref):
    blocks = [jnp.dot(xb[:, n * LRU_BLOCK_DIM:(n + 1) * LRU_BLOCK_DIM], w_ref[n],
                      preferred_element_type=F32) + b_ref[n]
              for n in range(xb.shape[1] // LRU_BLOCK_DIM)]
    return jnp.concatenate(blocks, axis=1)


def _lru_scan(xc, r_logit, i_logit, gate, h0, log_sig):
    ts, c = xc.shape
    r = jax.nn.sigmoid(r_logit)
    i = jax.nn.sigmoid(i_logit)
    log_a = LRU_C * r * log_sig
    a = jnp.exp(log_a)
    b = jnp.sqrt(-jnp.tanh(log_a) * (1.0 + a * a)) * (i * xc)

    groups = ts // SUBLANES
    a = a.reshape(groups, SUBLANES, c)
    b = b.reshape(groups, SUBLANES, c)
    frame = lax.broadcasted_iota(jnp.int32, (1, SUBLANES, c), 1)
    d = 1
    while d < SUBLANES:
        keep = frame >= d
        b = b + a * jnp.where(keep, pltpu.roll(b, d, axis=1), 0.0)
        a = a * jnp.where(keep, pltpu.roll(a, d, axis=1), 1.0)
        d *= 2
    carry = h0
    hs = []
    for gi in range(groups):
        hg = b[gi] + a[gi] * carry
        carry = hg[SUBLANES - 1:SUBLANES, :]
        hs.append(hg)
    return jnp.concatenate(hs, axis=0) * jax.nn.gelu(gate, approximate=True), carry


def _rec_branch_kernel(h_ref, wx_ref, wg_ref, cw_ref, cb_ref, wr_ref, br_ref, wi_ref,
                       bi_ref, lam_ref, o_ref, xr_ref, gr_ref, xc_ref, rl_ref, il_ref,
                       tail_ref, state_ref, *, nj, tiles_per_seq):
    s = pl.program_id(0)
    prev = jnp.maximum(s - 1, 0)
    ip = prev // nj
    jp = lax.rem(prev, nj)
    ts = xr_ref.shape[0]

    @pl.when(s == 0)
    def _():
        xr_ref[...] = jnp.zeros_like(xr_ref)
        gr_ref[...] = jnp.zeros_like(gr_ref)
        tail_ref[...] = jnp.zeros_like(tail_ref)
        state_ref[...] = jnp.zeros_like(state_ref)

    seq_start = lax.rem(ip, tiles_per_seq) == 0
    tail = jnp.where(seq_start, 0.0, tail_ref[jp])
    carry = jnp.where(seq_start, 0.0, state_ref[jp, 0:1, :])
    cw, cb = cw_ref[...], cb_ref[...]
    slabs = [(r0, r0 + LRU_ROW_SLAB) for r0 in range(0, ts, LRU_ROW_SLAB)]

    for r0, r1 in slabs:
        before = tail if r0 == 0 else xr_ref[r0 - SUBLANES:r0, :]
        xc_ref[r0:r1, :] = _causal_conv(xr_ref[r0:r1, :], before, cw, cb)
    tail_ref[jp] = xr_ref[ts - SUBLANES:ts, :]

    xr_ref[...] = jnp.dot(h_ref[...], wx_ref[...], preferred_element_type=F32)
    xb = xc_ref[...].astype(BF16)
    rl_ref[...] = _block_gate_logits(xb, wr_ref, br_ref)
    il_ref[...] = _block_gate_logits(xb, wi_ref, bi_ref)

    lam = lam_ref[...]
    log_sig = jnp.minimum(lam, 0.0) - jnp.log1p(jnp.exp(-jnp.abs(lam)))
    for r0, r1 in slabs:
        rec, carry = _lru_scan(xc_ref[r0:r1, :], rl_ref[r0:r1, :], il_ref[r0:r1, :],
                               gr_ref[r0:r1, :], carry, log_sig)
        o_ref[r0:r1, :] = rec.astype(o_ref.dtype)
    state_ref[jp] = jnp.broadcast_to(carry, state_ref.shape[1:])
    gr_ref[...] = jnp.dot(h_ref[...], wg_ref[...], preferred_element_type=F32)


def _rec_branch(h, w_x, w_g, conv_w, conv_b, w_r, b_r, w_i, b_i, lam, *, seq,
                tm=LRU_TILE, tc=LRU_CHANNEL_TILE):
    m, k = h.shape
    d_lru = LRU_BLOCKS * LRU_BLOCK_DIM
    assert seq % tm == 0 and d_lru % tc == 0 and w_x.shape == w_g.shape == (k, d_lru)
    nj = d_lru // tc
    n_tiles = (m // tm) * nj
    nblk = tc // LRU_BLOCK_DIM
    this_i = lambda s: jnp.minimum(s, n_tiles - 1) // nj
    this_j = lambda s: lax.rem(jnp.minimum(s, n_tiles - 1), nj)
    prev_i = lambda s: jnp.maximum(s - 1, 0) // nj
    prev_j = lambda s: lax.rem(jnp.maximum(s - 1, 0), nj)
    vec = lambda s: (0, prev_j(s))
    blk = lambda s: (prev_j(s), 0, 0)
    return pl.pallas_call(
        functools.partial(_rec_branch_kernel, nj=nj, tiles_per_seq=seq // tm),
        grid=(n_tiles + 1,),
        in_specs=[
            pl.BlockSpec((tm, k), lambda s: (this_i(s), 0)),
            pl.BlockSpec((k, tc), lambda s: (0, this_j(s))),
            pl.BlockSpec((k, tc), lambda s: (0, this_j(s))),
            pl.BlockSpec((CONV_WIDTH, tc), vec),
            pl.BlockSpec((1, tc), vec),
            pl.BlockSpec((nblk, LRU_BLOCK_DIM, LRU_BLOCK_DIM), blk),
            pl.BlockSpec((nblk, 1, LRU_BLOCK_DIM), blk),
            pl.BlockSpec((nblk, LRU_BLOCK_DIM, LRU_BLOCK_DIM), blk),
            pl.BlockSpec((nblk, 1, LRU_BLOCK_DIM), blk),
            pl.BlockSpec((1, tc), vec),
        ],
        out_specs=pl.BlockSpec((tm, tc), lambda s: (prev_i(s), prev_j(s))),
        out_shape=jax.ShapeDtypeStruct((m, d_lru), BF16),
        scratch_shapes=[pltpu.VMEM((tm, tc), F32), pltpu.VMEM((tm, tc), F32),
                        pltpu.VMEM((tm, tc), F32),
                        pltpu.VMEM((tm, tc), F32), pltpu.VMEM((tm, tc), F32),
                        pltpu.VMEM((nj, SUBLANES, tc), F32),
                        pltpu.VMEM((nj, SUBLANES, tc), F32)],
        compiler_params=_params("arbitrary"),
        name="rec_branch",
    )(h, w_x, w_g, conv_w, conv_b, w_r, b_r, w_i, b_i, lam)


def _out_proj_kernel(a_ref, r_ref, wa_ref, wr_ref, x_ref, g_ref, o_ref, xg_ref, rstd_ref, ss_ref,
                     *, n):
    j = pl.program_id(1)
    acc = jnp.dot(a_ref[...], wa_ref[...], preferred_element_type=F32) + x_ref[...]
    x1 = jnp.dot(r_ref[...], wr_ref[...], preferred_element_type=F32) + acc
    o_ref[...] = x1
    xg_ref[...] = (x1 * g_ref[...]).astype(xg_ref.dtype)
    ss = jnp.sum(x1 * x1, axis=-1, keepdims=True)

    @pl.when(j == 0)
    def _():
        ss_ref[...] = ss

    @pl.when(j > 0)
    def _():
        ss_ref[...] += ss

    @pl.when(j == pl.num_programs(1) - 1)
    def _():
        rstd_ref[...] = lax.rsqrt(ss_ref[...] * (1.0 / n) + EPS)


def _out_proj(attn, rec, w_out, x, g_next, *, tm, tn):
    m, ka = attn.shape
    kr = rec.shape[1]
    n = w_out.shape[1]
    assert ka == kr
    tile = lambda i, j: (i, j)
    return pl.pallas_call(
        functools.partial(_out_proj_kernel, n=n),
        grid=(m // tm, n // tn),
        in_specs=[
            pl.BlockSpec((tm, ka), lambda i, j: (i, 0)),
            pl.BlockSpec((tm, kr), lambda i, j: (i, 0)),
            pl.BlockSpec((ka, tn), lambda i, j: (0, j)),
            pl.BlockSpec((kr, tn), lambda i, j: (1, j)),
            pl.BlockSpec((tm, tn), tile),
            pl.BlockSpec((1, tn), lambda i, j: (0, j)),
        ],
        out_specs=[pl.BlockSpec((tm, tn), tile), pl.BlockSpec((tm, tn), tile),
                   pl.BlockSpec((tm, 1), lambda i, j: (i, 0))],
        out_shape=[jax.ShapeDtypeStruct((m, n), F32), jax.ShapeDtypeStruct((m, n), BF16),
                   jax.ShapeDtypeStruct((m, 1), F32)],
        scratch_shapes=[pltpu.VMEM((tm, 1), F32)],
        compiler_params=_params("arbitrary", "arbitrary"),
        name="out_proj",
    )(attn, rec, w_out, w_out, x, g_next)


def _rowscaled_matmul_kernel(xg_ref, rstd_ref, w_ref, o_ref):
    acc = jnp.dot(xg_ref[...], w_ref[...], preferred_element_type=F32) * rstd_ref[...]
    o_ref[...] = jnp.square(jnp.maximum(acc, 0.0)).astype(o_ref.dtype)


def _rowscaled_relu2_matmul(xg, rstd, w, *, out_dtype, tm, tn):
    m, k = xg.shape
    n = w.shape[1]
    return pl.pallas_call(
        _rowscaled_matmul_kernel,
        grid=(m // tm, n // tn),
        in_specs=[
            pl.BlockSpec((tm, k), lambda i, j: (i, 0)),
            pl.BlockSpec((tm, 1), lambda i, j: (i, 0)),
            pl.BlockSpec((k, tn), lambda i, j: (0, j)),
        ],
        out_specs=pl.BlockSpec((tm, tn), lambda i, j: (i, j)),
        out_shape=jax.ShapeDtypeStruct((m, n), out_dtype),
        compiler_params=_params("arbitrary", "arbitrary"),
        name="up_proj",
    )(xg, rstd, w)


def _down_norm_kernel(u_ref, w_ref, x_ref, g_ref, o_ref):
    k = pl.program_id(1)
    half = x_ref.shape[1]

    @pl.when(k == 0)
    def _():
        o_ref[:, :half] = x_ref[...]
        o_ref[:, half:] = jnp.zeros((o_ref.shape[0], o_ref.shape[1] - half), F32)

    o_ref[...] = jnp.dot(u_ref[...], w_ref[...], preferred_element_type=F32) + o_ref[...]

    @pl.when(k == 1)
    def _():
        o_ref[:, half:] += x_ref[...]

    @pl.when(k == pl.num_programs(1) - 1)
    def _():
        for r0 in range(0, o_ref.shape[0], NORM_SLAB_ROWS):
            rows = slice(r0, r0 + NORM_SLAB_ROWS)
            o_ref[rows, :] = _rmsnorm_rows(o_ref[rows, :], g_ref[...])


def _down_norm(u, w, x, g, *, tm, tk):
    m, kk = u.shape
    n = w.shape[1]
    assert kk // tk >= 2
    return pl.pallas_call(
        _down_norm_kernel,
        grid=(m // tm, kk // tk),
        in_specs=[
            pl.BlockSpec((tm, tk), lambda i, k: (i, k)),
            pl.BlockSpec((tk, n), lambda i, k: (k, 0)),
            pl.BlockSpec((tm, n // 2), lambda i, k: (i, jnp.minimum(k, 1))),
            pl.BlockSpec((1, n), lambda i, k: (0, 0)),
        ],
        out_specs=pl.BlockSpec((tm, n), lambda i, k: (i, 0), pipeline_mode=pl.Buffered(1)),
        out_shape=jax.ShapeDtypeStruct((m, n), F32),
        compiler_params=_params("arbitrary", "arbitrary"),
        name="down_norm",
    )(u, w, x, g)


def kernel(x, norm_mix_g, w_in, conv_w, conv_b, w_rgate, b_rgate, w_igate, b_igate, lru_lambda, lambda_q1, lambda_k1, lambda_q2, lambda_k2, subln_g, w_out, norm_mlp_g, w_mlp_up, w_mlp_down, norm_final_g):
    batch, seq, d_model = x.shape
    depth = w_in.shape[0]
    d_attn = ATTN_HEADS * HEAD_V_DIM
    d_lru = LRU_BLOCKS * LRU_BLOCK_DIM
    m = batch * seq
    row = lambda v: v.reshape(1, -1)

    xf = x.reshape(m, d_model)
    for l in range(depth):
        lambda_init = 0.8 - 0.6 * math.exp(-0.3 * l)
        g_mix = row(norm_mix_g[l])
        w_qkv = w_in[l][:, :3 * d_attn].astype(BF16)
        qkv, h_mix = _norm_matmul(xf, g_mix, w_qkv, out_dtype=BF16, relu2=False,
                                  scaled_cols=d_attn, scale=ATTN_Q_PRESCALE,
                                  return_normed=True, **IN_PROJ_TILES)

        lam = _diff_lambda(row(lambda_q1[l]), row(lambda_k1[l]), row(lambda_q2[l]),
                           row(lambda_k2[l]), lambda_init)
        full = lambda w: (w, 0, w.shape[1])
        attn, w_x_l, w_g_l, w_out_l, w_up_l, w_down_l = _diff_attention(
            qkv.reshape(batch, seq, 3 * d_attn), lam, subln_g[l].reshape(HEAD_V_DIM, 1),
            ((w_in[l], 3 * d_attn, d_lru), (w_in[l], 3 * d_attn + d_lru, d_lru),
             full(w_out[l]), full(w_mlp_up[l]), full(w_mlp_down[l])),
            batch=batch, seq=seq, out_scale=1.0 - lambda_init)

        rec = _rec_branch(
            h_mix, w_x_l, w_g_l, conv_w[l], row(conv_b[l]),
            w_rgate[l].astype(BF16), b_rgate[l].reshape(LRU_BLOCKS, 1, LRU_BLOCK_DIM),
            w_igate[l].astype(BF16), b_igate[l].reshape(LRU_BLOCKS, 1, LRU_BLOCK_DIM),
            row(lru_lambda[l]), seq=seq)

        xf, xg, rstd = _out_proj(attn.reshape(m, d_attn), rec, w_out_l, xf, row(norm_mlp_g[l]),
                                 **OUT_PROJ_TILES)

        u = _rowscaled_relu2_matmul(xg, rstd, w_up_l, out_dtype=BF16, **UP_PROJ_TILES)
        last = l == depth - 1
        assert last, "DEPTH > 1 needs an un-normalised down-projection variant"
        xf = _down_norm(u, w_down_l, xf, row(norm_final_g), **DOWN_PROJ_TILES)
    return xf.reshape(batch, seq, d_model)
```

```python
import functools
import math

import jax
import jax.numpy as jnp
from jax import lax
from jax.experimental import pallas as pl
from jax.experimental.pallas import tpu as pltpu

F32 = jnp.float32
BF16 = jnp.bfloat16

EPS = 1e-6
CHUNK = 64
ATTN_HEADS = 16
HEAD_V_DIM = 128
HEAD_QK_DIM = 64
LRU_BLOCKS = 16
LRU_BLOCK_DIM = 128
CONV_WIDTH = 4
LRU_C = 8.0

V7X_VMEM_LIMIT_BYTES = 56 * 1024 * 1024
SUBLANES = 8
BF16_ROW_TILE = 16

NORM_PREP_ROWS = 512
QKV_PROJ_TILES = dict(tm=1024, tn=1024)
OUT_PROJ_TILES = dict(tm=1024, tn=512)
UP_PROJ_TILES = dict(tm=1024, tn=1024)
DOWN_PROJ_TILES = dict(tm=512, tk=2048)
ATTN_TILE = 512
ATTN_HEADS_PER_STEP = 4
LRU_TILE = 512
LRU_CHANNEL_TILE = 256
LRU_ROW_SLAB = 32
NORM_SLAB_ROWS = 32


def _params(*semantics):
    return pltpu.CompilerParams(dimension_semantics=semantics,
                                vmem_limit_bytes=V7X_VMEM_LIMIT_BYTES)


def _rmsnorm_rows(x, g):
    ms = jnp.mean(x * x, axis=-1, keepdims=True)
    return (x * lax.rsqrt(ms + EPS)) * g


def _lambda_kernel(q1_ref, k1_ref, q2_ref, k2_ref, o_ref, *, lambda_init):
    d1 = jnp.sum(q1_ref[...] * k1_ref[...], axis=-1, keepdims=True)
    d2 = jnp.sum(q2_ref[...] * k2_ref[...], axis=-1, keepdims=True)
    o_ref[...] = jnp.exp(d1) - jnp.exp(d2) + lambda_init


def _diff_lambda(q1, k1, q2, k2, lambda_init):
    return pl.pallas_call(
        functools.partial(_lambda_kernel, lambda_init=lambda_init),
        out_shape=jax.ShapeDtypeStruct((1, 1), F32),
        name="diff_lambda",
    )(q1, k1, q2, k2)


ATTN_ONES_ROWS = 16
ATTN_Q_PRESCALE = HEAD_QK_DIM ** -0.5 * math.log2(math.e)


def _attn_kernel(lam_ref, g_ref, q_ref, k_ref, v_ref, *refs, tq, n_tiles, heads, out_scale,
                 n_cast):
    cast_in, o_ref, cast_out = refs[:n_cast], refs[n_cast], refs[n_cast + 1:2 * n_cast + 1]
    vt_ref, qm_ref, acc_ref, m_ref, s_ref = refs[2 * n_cast + 1:]
    qi = pl.program_id(2)
    dv = HEAD_V_DIM
    head_lanes = lambda h: slice(h * dv, (h + 1) * dv)

    for w_ref, wb_ref in zip(cast_in, cast_out):
        wb_ref[...] = w_ref[...].astype(BF16)

    @pl.when(qi == 0)
    def _():
        for h in range(heads):
            for j in range(n_tiles):
                vt = v_ref[0, j * tq:(j + 1) * tq, head_lanes(h)].astype(F32).T
                vt_ref[h, j, 0:dv, :] = vt.astype(BF16)
                vt_ref[h, j, dv:dv + ATTN_ONES_ROWS, :] = jnp.ones((ATTN_ONES_ROWS, tq), BF16)

    lane = lax.broadcasted_iota(jnp.int32, (tq, dv), 1)
    for h in range(heads):
        q = q_ref[0, :, head_lanes(h)]
        zero = jnp.zeros_like(q)
        qm_ref[h, 0] = jnp.where(lane < HEAD_QK_DIM, q, zero)
        qm_ref[h, 1] = jnp.where(lane >= HEAD_QK_DIM, q, zero)
    acc_ref[...] = jnp.zeros_like(acc_ref)
    m_ref[...] = jnp.full(m_ref.shape, -jnp.inf, F32)
    nt = (((1,), (1,)), ((), ()))

    def scores_into(s_ref, j, h):
        k = k_ref[0, pl.ds(pl.multiple_of(j * tq, tq), tq), head_lanes(h)]
        for mi in range(2):
            s_ref[mi] = lax.dot_general(k, qm_ref[h, mi], nt, preferred_element_type=F32)

    def softmax_pv(s_ref, j, h, masked):
        vt = vt_ref[h, j]
        if masked:
            key_chunk = lax.broadcasted_iota(jnp.int32, (tq, tq), 0) // CHUNK
            qry_chunk = lax.broadcasted_iota(jnp.int32, (tq, tq), 1) // CHUNK
            visible = key_chunk <= qry_chunk
            read = lambda mi: jnp.where(visible, s_ref[mi], -jnp.inf)
        else:
            read = lambda mi: s_ref[mi]
        for mi in range(2):
            m_old = m_ref[h, mi, 0:1, :]
            m_new = jnp.maximum(m_old, jnp.max(read(mi), axis=0, keepdims=True))
            alpha = jnp.exp2(m_old - m_new)
            p = jnp.exp2(read(mi) - m_new).astype(BF16)
            pv = jnp.dot(vt, p, preferred_element_type=F32)
            acc_ref[h, mi] = alpha * acc_ref[h, mi] + pv
            m_ref[h, mi] = jnp.broadcast_to(m_new, m_ref.shape[2:])

    scores_into(s_ref.at[0], 0, 0)

    def one_tile(j, masked):
        for h in range(heads):
            if h + 1 < heads:
                scores_into(s_ref.at[h + 1], j, h + 1)
            elif not masked:
                scores_into(s_ref.at[0], j + 1, 0)
            softmax_pv(s_ref.at[h], j, h, masked)

    def two_tiles(t, carry):
        one_tile(2 * t, False)
        one_tile(2 * t + 1, False)
        return carry

    lax.fori_loop(0, lax.shift_right_logical(qi, 1), two_tiles, 0)

    @pl.when(lax.bitwise_and(qi, 1) == 1)
    def _():
        one_tile(qi - 1, False)

    one_tile(qi, True)

    lam = lam_ref[0, 0]
    for h in range(heads):
        o1 = acc_ref[h, 0, 0:dv, :] * (1.0 / acc_ref[h, 0, dv:dv + 1, :])
        o2 = acc_ref[h, 1, 0:dv, :] * (lam / acc_ref[h, 1, dv:dv + 1, :])
        a = o1 - o2
        ms = jnp.mean(a * a, axis=0, keepdims=True)
        y = ((a * lax.rsqrt(ms + EPS)) * g_ref[...]) * out_scale
        o_ref[0, :, head_lanes(h)] = y.T.astype(o_ref.dtype)


def _diff_attention(qkv, lam, subln_g_col, cast_weights=(), *, batch, seq, out_scale,
                    tq=ATTN_TILE, heads=ATTN_HEADS_PER_STEP):
    assert ATTN_HEADS % heads == 0
    groups = ATTN_HEADS // heads
    n_tiles = seq // tq
    acc_rows = HEAD_V_DIM + ATTN_ONES_ROWS
    width = heads * HEAD_V_DIM
    n_steps = batch * groups * n_tiles
    step = lambda b, g, i: (b * groups + g) * n_tiles + i
    cast_in_specs, cast_out_specs, cast_shapes = [], [], []
    for w, col0, ncols in cast_weights:
        assert w.shape[0] % (n_steps * BF16_ROW_TILE) == 0 and col0 % ncols == 0
        rows = w.shape[0] // n_steps
        cast_in_specs.append(pl.BlockSpec(
            (rows, ncols), lambda b, g, i, cb=col0 // ncols: (step(b, g, i), cb)))
        cast_out_specs.append(pl.BlockSpec((rows, ncols), lambda b, g, i: (step(b, g, i), 0)))
        cast_shapes.append(jax.ShapeDtypeStruct((w.shape[0], ncols), BF16))
    return pl.pallas_call(
        functools.partial(_attn_kernel, tq=tq, n_tiles=n_tiles, heads=heads, out_scale=out_scale,
                          n_cast=len(cast_weights)),
        grid=(batch, groups, n_tiles),
        in_specs=[
            pl.BlockSpec(memory_space=pltpu.SMEM),
            pl.BlockSpec((HEAD_V_DIM, 1), lambda b, g, i: (0, 0)),
            pl.BlockSpec((1, tq, width), lambda b, g, i: (b, i, g)),
            pl.BlockSpec((1, seq, width), lambda b, g, i: (b, 0, groups + g)),
            pl.BlockSpec((1, seq, width), lambda b, g, i: (b, 0, 2 * groups + g)),
        ] + cast_in_specs,
        out_specs=[pl.BlockSpec((1, tq, width), lambda b, g, i: (b, i, g))] + cast_out_specs,
        out_shape=[jax.ShapeDtypeStruct((batch, seq, ATTN_HEADS * HEAD_V_DIM), BF16)]
        + cast_shapes,
        scratch_shapes=[pltpu.VMEM((heads, n_tiles, acc_rows, tq), BF16),
                        pltpu.VMEM((heads, 2, tq, HEAD_V_DIM), BF16),
                        pltpu.VMEM((heads, 2, acc_rows, tq), F32),
                        pltpu.VMEM((heads, 2, SUBLANES, tq), F32),
                        pltpu.VMEM((heads, 2, tq, tq), F32)],
        compiler_params=_params("arbitrary", "arbitrary", "arbitrary"),
        name="diff_attention",
    )(lam, subln_g_col, qkv, qkv, qkv, *[w for w, _, _ in cast_weights])


def _causal_conv(x, tail, cw, cb):
    ts = x.shape[0]
    xp = jnp.concatenate([tail, x], axis=0)
    xc = cb
    for j in range(CONV_WIDTH):
        off = SUBLANES - (CONV_WIDTH - 1) + j
        xc = xc + cw[j:j + 1, :] * xp[off:off + ts]
    return xc


def _block_gate_logits(xb, w_ref, b_ref):
    blocks = [jnp.dot(xb[:, n * LRU_BLOCK_DIM:(n + 1) * LRU_BLOCK_DIM], w_ref[n],
                      preferred_element_type=F32) + b_ref[n]
              for n in range(xb.shape[1] // LRU_BLOCK_DIM)]
    return jnp.concatenate(blocks, axis=1)


def _lru_scan(xc, r_logit, i_logit, gate, h0, log_sig):
    ts, c = xc.shape
    r = jax.nn.sigmoid(r_logit)
    i = jax.nn.sigmoid(i_logit)
    log_a = LRU_C * r * log_sig
    a = jnp.exp(log_a)
    b = jnp.sqrt(-jnp.tanh(log_a) * (1.0 + a * a)) * (i * xc)

    groups = ts // SUBLANES
    a = a.reshape(groups, SUBLANES, c)
    b = b.reshape(groups, SUBLANES, c)
    frame = lax.broadcasted_iota(jnp.int32, (1, SUBLANES, c), 1)
    d = 1
    while d < SUBLANES:
        keep = frame >= d
        b = b + a * jnp.where(keep, pltpu.roll(b, d, axis=1), 0.0)
        a = a * jnp.where(keep, pltpu.roll(a, d, axis=1), 1.0)
        d *= 2
    carry = h0
    hs = []
    for gi in range(groups):
        hg = b[gi] + a[gi] * carry
        carry = hg[SUBLANES - 1:SUBLANES, :]
        hs.append(hg)
    return jnp.concatenate(hs, axis=0) * jax.nn.gelu(gate, approximate=True), carry


def _rec_branch_kernel(h_ref, rstd_ref, wx_ref, wg_ref, cw_ref, cb_ref, wr_ref, br_ref, wi_ref,
                       bi_ref, lam_ref, o_ref, xr_ref, gr_ref, xc_ref, rl_ref, il_ref,
                       tail_ref, state_ref, *, nj, tiles_per_seq):
    s = pl.program_id(0)
    prev = jnp.maximum(s - 1, 0)
    ip = prev // nj
    jp = lax.rem(prev, nj)
    ts = xr_ref.shape[0]

    @pl.when(s == 0)
    def _():
        xr_ref[...] = jnp.zeros_like(xr_ref)
        gr_ref[...] = jnp.zeros_like(gr_ref)
        tail_ref[...] = jnp.zeros_like(tail_ref)
        state_ref[...] = jnp.zeros_like(state_ref)

    seq_start = lax.rem(ip, tiles_per_seq) == 0
    tail = jnp.where(seq_start, 0.0, tail_ref[jp])
    carry = jnp.where(seq_start, 0.0, state_ref[jp, 0:1, :])
    cw, cb = cw_ref[...], cb_ref[...]
    slabs = [(r0, r0 + LRU_ROW_SLAB) for r0 in range(0, ts, LRU_ROW_SLAB)]

    for r0, r1 in slabs:
        before = tail if r0 == 0 else xr_ref[r0 - SUBLANES:r0, :]
        xc_ref[r0:r1, :] = _causal_conv(xr_ref[r0:r1, :], before, cw, cb)
    tail_ref[jp] = xr_ref[ts - SUBLANES:ts, :]

    xr_ref[...] = jnp.dot(h_ref[...], wx_ref[...], preferred_element_type=F32) * rstd_ref[...]
    xb = xc_ref[...].astype(BF16)
    rl_ref[...] = _block_gate_logits(xb, wr_ref, br_ref)
    il_ref[...] = _block_gate_logits(xb, wi_ref, bi_ref)

    lam = lam_ref[...]
    log_sig = jnp.minimum(lam, 0.0) - jnp.log1p(jnp.exp(-jnp.abs(lam)))
    for r0, r1 in slabs:
        rec, carry = _lru_scan(xc_ref[r0:r1, :], rl_ref[r0:r1, :], il_ref[r0:r1, :],
                               gr_ref[r0:r1, :], carry, log_sig)
        o_ref[r0:r1, :] = rec.astype(o_ref.dtype)
    state_ref[jp] = jnp.broadcast_to(carry, state_ref.shape[1:])
    gr_ref[...] = jnp.dot(h_ref[...], wg_ref[...], preferred_element_type=F32) * rstd_ref[...]


def _rec_branch(h, rstd, w_x, w_g, conv_w, conv_b, w_r, b_r, w_i, b_i, lam, *, seq,
                tm=LRU_TILE, tc=LRU_CHANNEL_TILE):
    m, k = h.shape
    d_lru = LRU_BLOCKS * LRU_BLOCK_DIM
    assert seq % tm == 0 and d_lru % tc == 0 and w_x.shape == w_g.shape == (k, d_lru)
    nj = d_lru // tc
    n_tiles = (m // tm) * nj
    nblk = tc // LRU_BLOCK_DIM
    this_i = lambda s: jnp.minimum(s, n_tiles - 1) // nj
    this_j = lambda s: lax.rem(jnp.minimum(s, n_tiles - 1), nj)
    prev_i = lambda s: jnp.maximum(s - 1, 0) // nj
    prev_j = lambda s: lax.rem(jnp.maximum(s - 1, 0), nj)
    vec = lambda s: (0, prev_j(s))
    blk = lambda s: (prev_j(s), 0, 0)
    return pl.pallas_call(
        functools.partial(_rec_branch_kernel, nj=nj, tiles_per_seq=seq // tm),
        grid=(n_tiles + 1,),
        in_specs=[
            pl.BlockSpec((tm, k), lambda s: (this_i(s), 0)),
            pl.BlockSpec((tm, 1), lambda s: (this_i(s), 0)),
            pl.BlockSpec((k, tc), lambda s: (0, this_j(s))),
            pl.BlockSpec((k, tc), lambda s: (0, this_j(s))),
            pl.BlockSpec((CONV_WIDTH, tc), vec),
            pl.BlockSpec((1, tc), vec),
            pl.BlockSpec((nblk, LRU_BLOCK_DIM, LRU_BLOCK_DIM), blk),
            pl.BlockSpec((nblk, 1, LRU_BLOCK_DIM), blk),
            pl.BlockSpec((nblk, LRU_BLOCK_DIM, LRU_BLOCK_DIM), blk),
            pl.BlockSpec((nblk, 1, LRU_BLOCK_DIM), blk),
            pl.BlockSpec((1, tc), vec),
        ],
        out_specs=pl.BlockSpec((tm, tc), lambda s: (prev_i(s), prev_j(s))),
        out_shape=jax.ShapeDtypeStruct((m, d_lru), BF16),
        scratch_shapes=[pltpu.VMEM((tm, tc), F32), pltpu.VMEM((tm, tc), F32),
                        pltpu.VMEM((tm, tc), F32),
                        pltpu.VMEM((tm, tc), F32), pltpu.VMEM((tm, tc), F32),
                        pltpu.VMEM((nj, SUBLANES, tc), F32),
                        pltpu.VMEM((nj, SUBLANES, tc), F32)],
        compiler_params=_params("arbitrary"),
        name="rec_branch",
    )(h, rstd, w_x, w_g, conv_w, conv_b, w_r, b_r, w_i, b_i, lam)


def _out_proj_kernel(a_ref, r_ref, wa_ref, wr_ref, x_ref, g_ref, o_ref, xg_ref, rstd_ref, ss_ref,
                     *, n):
    j = pl.program_id(1)
    acc = jnp.dot(a_ref[...], wa_ref[...], preferred_element_type=F32) + x_ref[...]
    x1 = jnp.dot(r_ref[...], wr_ref[...], preferred_element_type=F32) + acc
    o_ref[...] = x1
    xg_ref[...] = (x1 * g_ref[...]).astype(xg_ref.dtype)
    ss = jnp.sum(x1 * x1, axis=-1, keepdims=True)

    @pl.when(j == 0)
    def _():
        ss_ref[...] = ss

    @pl.when(j > 0)
    def _():
        ss_ref[...] += ss

    @pl.when(j == pl.num_programs(1) - 1)
    def _():
        rstd_ref[...] = lax.rsqrt(ss_ref[...] * (1.0 / n) + EPS)


def _out_proj(attn, rec, w_out, x, g_next, *, tm, tn):
    m, ka = attn.shape
    kr = rec.shape[1]
    n = w_out.shape[1]
    assert ka == kr
    tile = lambda i, j: (i, j)
    return pl.pallas_call(
        functools.partial(_out_proj_kernel, n=n),
        grid=(m // tm, n // tn),
        in_specs=[
            pl.BlockSpec((tm, ka), lambda i, j: (i, 0)),
            pl.BlockSpec((tm, kr), lambda i, j: (i, 0)),
            pl.BlockSpec((ka, tn), lambda i, j: (0, j)),
            pl.BlockSpec((kr, tn), lambda i, j: (1, j)),
            pl.BlockSpec((tm, tn), tile),
            pl.BlockSpec((1, tn), lambda i, j: (0, j)),
        ],
        out_specs=[pl.BlockSpec((tm, tn), tile), pl.BlockSpec((tm, tn), tile),
                   pl.BlockSpec((tm, 1), lambda i, j: (i, 0))],
        out_shape=[jax.ShapeDtypeStruct((m, n), F32), jax.ShapeDtypeStruct((m, n), BF16),
                   jax.ShapeDtypeStruct((m, 1), F32)],
        scratch_shapes=[pltpu.VMEM((tm, 1), F32)],
        compiler_params=_params("arbitrary", "arbitrary"),
        name="out_proj",
    )(attn, rec, w_out, w_out, x, g_next)


def _rowscaled_matmul_kernel(xg_ref, rstd_ref, w_ref, o_ref, *, relu2, scaled_tiles, scale):
    acc = jnp.dot(xg_ref[...], w_ref[...], preferred_element_type=F32) * rstd_ref[...]
    if relu2:
        acc = jnp.square(jnp.maximum(acc, 0.0))
    if scaled_tiles:
        acc = acc * jnp.where(pl.program_id(1) < scaled_tiles, scale, 1.0).astype(F32)
    o_ref[...] = acc.astype(o_ref.dtype)


def _rowscaled_matmul(xg, rstd, w, *, out_dtype, tm, tn, name, relu2=False, scaled_cols=0,
                      scale=1.0):
    m, k = xg.shape
    n = w.shape[1]
    assert scaled_cols % tn == 0
    return pl.pallas_call(
        functools.partial(_rowscaled_matmul_kernel, relu2=relu2, scaled_tiles=scaled_cols // tn,
                          scale=scale),
        grid=(m // tm, n // tn),
        in_specs=[
            pl.BlockSpec((tm, k), lambda i, j: (i, 0)),
            pl.BlockSpec((tm, 1), lambda i, j: (i, 0)),
            pl.BlockSpec((k, tn), lambda i, j: (0, j)),
        ],
        out_specs=pl.BlockSpec((tm, tn), lambda i, j: (i, j)),
        out_shape=jax.ShapeDtypeStruct((m, n), out_dtype),
        compiler_params=_params("arbitrary", "arbitrary"),
        name=name,
    )(xg, rstd, w)


def _norm_prep_kernel(x_ref, g_ref, xg_ref, rstd_ref):
    for r0 in range(0, x_ref.shape[0], NORM_SLAB_ROWS):
        rows = slice(r0, r0 + NORM_SLAB_ROWS)
        x = x_ref[rows, :]
        xg_ref[rows, :] = (x * g_ref[...]).astype(xg_ref.dtype)
        rstd_ref[rows, :] = lax.rsqrt(jnp.mean(x * x, axis=-1, keepdims=True) + EPS)


def _norm_prep(x, g, *, tm):
    m, k = x.shape
    return pl.pallas_call(
        _norm_prep_kernel,
        grid=(m // tm,),
        in_specs=[pl.BlockSpec((tm, k), lambda i: (i, 0)), pl.BlockSpec((1, k), lambda i: (0, 0))],
        out_specs=[pl.BlockSpec((tm, k), lambda i: (i, 0)), pl.BlockSpec((tm, 1), lambda i: (i, 0))],
        out_shape=[jax.ShapeDtypeStruct((m, k), BF16), jax.ShapeDtypeStruct((m, 1), F32)],
        compiler_params=_params("arbitrary"),
        name="norm_prep",
    )(x, g)


def _down_norm_kernel(u_ref, w_ref, x_ref, g_ref, o_ref):
    k = pl.program_id(1)
    half = x_ref.shape[1]

    @pl.when(k == 0)
    def _():
        o_ref[:, :half] = x_ref[...]
        o_ref[:, half:] = jnp.zeros((o_ref.shape[0], o_ref.shape[1] - half), F32)

    o_ref[...] = jnp.dot(u_ref[...], w_ref[...], preferred_element_type=F32) + o_ref[...]

    @pl.when(k == 1)
    def _():
        o_ref[:, half:] += x_ref[...]

    @pl.when(k == pl.num_programs(1) - 1)
    def _():
        for r0 in range(0, o_ref.shape[0], NORM_SLAB_ROWS):
            rows = slice(r0, r0 + NORM_SLAB_ROWS)
            o_ref[rows, :] = _rmsnorm_rows(o_ref[rows, :], g_ref[...])


def _down_norm(u, w, x, g, *, tm, tk):
    m, kk = u.shape
    n = w.shape[1]
    assert kk // tk >= 2
    return pl.pallas_call(
        _down_norm_kernel,
        grid=(m // tm, kk // tk),
        in_specs=[
            pl.BlockSpec((tm, tk), lambda i, k: (i, k)),
            pl.BlockSpec((tk, n), lambda i, k: (k, 0)),
            pl.BlockSpec((tm, n // 2), lambda i, k: (i, jnp.minimum(k, 1))),
            pl.BlockSpec((1, n), lambda i, k: (0, 0)),
        ],
        out_specs=pl.BlockSpec((tm, n), lambda i, k: (i, 0), pipeline_mode=pl.Buffered(1)),
        out_shape=jax.ShapeDtypeStruct((m, n), F32),
        compiler_params=_params("arbitrary", "arbitrary"),
        name="down_norm",
    )(u, w, x, g)


def kernel(x, norm_mix_g, w_in, conv_w, conv_b, w_rgate, b_rgate, w_igate, b_igate, lru_lambda, lambda_q1, lambda_k1, lambda_q2, lambda_k2, subln_g, w_out, norm_mlp_g, w_mlp_up, w_mlp_down, norm_final_g):
    batch, seq, d_model = x.shape
    depth = w_in.shape[0]
    d_attn = ATTN_HEADS * HEAD_V_DIM
    d_lru = LRU_BLOCKS * LRU_BLOCK_DIM
    m = batch * seq
    row = lambda v: v.reshape(1, -1)

    xf = x.reshape(m, d_model)
    for l in range(depth):
        lambda_init = 0.8 - 0.6 * math.exp(-0.3 * l)
        xg_mix, rstd_mix = _norm_prep(xf, row(norm_mix_g[l]), tm=NORM_PREP_ROWS)
        w_qkv = w_in[l][:, :3 * d_attn].astype(BF16)
        qkv = _rowscaled_matmul(xg_mix, rstd_mix, w_qkv, out_dtype=BF16, scaled_cols=d_attn,
                                scale=ATTN_Q_PRESCALE, name="qkv_proj", **QKV_PROJ_TILES)

        lam = _diff_lambda(row(lambda_q1[l]), row(lambda_k1[l]), row(lambda_q2[l]),
                           row(lambda_k2[l]), lambda_init)
        full = lambda w: (w, 0, w.shape[1])
        attn, w_x_l, w_g_l, w_out_l, w_up_l, w_down_l = _diff_attention(
            qkv.reshape(batch, seq, 3 * d_attn), lam, subln_g[l].reshape(HEAD_V_DIM, 1),
            ((w_in[l], 3 * d_attn, d_lru), (w_in[l], 3 * d_attn + d_lru, d_lru),
             full(w_out[l]), full(w_mlp_up[l]), full(w_mlp_down[l])),
            batch=batch, seq=seq, out_scale=1.0 - lambda_init)

        rec = _rec_branch(
            xg_mix, rstd_mix, w_x_l, w_g_l, conv_w[l], row(conv_b[l]),
            w_rgate[l].astype(BF16), b_rgate[l].reshape(LRU_BLOCKS, 1, LRU_BLOCK_DIM),
            w_igate[l].astype(BF16), b_igate[l].reshape(LRU_BLOCKS, 1, LRU_BLOCK_DIM),
            row(lru_lambda[l]), seq=seq)

        xf, xg, rstd = _out_proj(attn.reshape(m, d_attn), rec, w_out_l, xf, row(norm_mlp_g[l]),
                                 **OUT_PROJ_TILES)

        u = _rowscaled_matmul(xg, rstd, w_up_l, out_dtype=BF16, relu2=True, name="up_proj",
                              **UP_PROJ_TILES)
        last = l == depth - 1
        assert last, "DEPTH > 1 needs an un-normalised down-projection variant"
        xf = _down_norm(u, w_down_l, xf, row(norm_final_g), **DOWN_PROJ_TILES)
    return xf.reshape(batch, seq, d_model)
```

```python
import functools
import math

import jax
import jax.numpy as jnp
from jax import lax
from jax.experimental import pallas as pl
from jax.experimental.pallas import tpu as pltpu

F32 = jnp.float32
BF16 = jnp.bfloat16

EPS = 1e-6
CHUNK = 64
ATTN_HEADS = 16
HEAD_V_DIM = 128
HEAD_QK_DIM = 64
LRU_BLOCKS = 16
LRU_BLOCK_DIM = 128
CONV_WIDTH = 4
LRU_C = 8.0

V7X_VMEM_LIMIT_BYTES = 56 * 1024 * 1024
SUBLANES = 8
BF16_ROW_TILE = 16

NORM_PREP_ROWS = 512
QKV_PROJ_TILES = dict(tm=1024, tn=1024)
OUT_PROJ_TILES = dict(tm=1024, tn=512)
UP_PROJ_TILES = dict(tm=1024, tn=1024)
DOWN_PROJ_TILES = dict(tm=512, tk=2048)
ATTN_TILE = 512
ATTN_HEADS_PER_STEP = 4
LRU_TILE = 512
LRU_CHANNEL_TILE = 256
LRU_ROW_SLAB = 32
NORM_SLAB_ROWS = 32


def _params(*semantics):
    return pltpu.CompilerParams(dimension_semantics=semantics,
                                vmem_limit_bytes=V7X_VMEM_LIMIT_BYTES)


def _rmsnorm_rows(x, g):
    ms = jnp.mean(x * x, axis=-1, keepdims=True)
    return (x * lax.rsqrt(ms + EPS)) * g


def _lambda_kernel(q1_ref, k1_ref, q2_ref, k2_ref, o_ref, *, lambda_init):
    d1 = jnp.sum(q1_ref[...] * k1_ref[...], axis=-1, keepdims=True)
    d2 = jnp.sum(q2_ref[...] * k2_ref[...], axis=-1, keepdims=True)
    o_ref[...] = jnp.exp(d1) - jnp.exp(d2) + lambda_init


def _diff_lambda(q1, k1, q2, k2, lambda_init):
    return pl.pallas_call(
        functools.partial(_lambda_kernel, lambda_init=lambda_init),
        out_shape=jax.ShapeDtypeStruct((1, 1), F32),
        name="diff_lambda",
    )(q1, k1, q2, k2)


ATTN_ONES_ROWS = 16
ATTN_Q_PRESCALE = HEAD_QK_DIM ** -0.5 * math.log2(math.e)


def _attn_kernel(lam_ref, g_ref, q_ref, k_ref, v_ref, *refs, tq, n_tiles, heads, out_scale,
                 n_cast):
    cast_in, o_ref, cast_out = refs[:n_cast], refs[n_cast], refs[n_cast + 1:2 * n_cast + 1]
    vt_ref, qm_ref, acc_ref, m_ref, s_ref = refs[2 * n_cast + 1:]
    qi = pl.program_id(2)
    dv = HEAD_V_DIM
    head_lanes = lambda h: slice(h * dv, (h + 1) * dv)

    for w_ref, wb_ref in zip(cast_in, cast_out):
        wb_ref[...] = w_ref[...].astype(BF16)

    @pl.when(qi == 0)
    def _():
        for h in range(heads):
            for j in range(n_tiles):
                vt = v_ref[0, j * tq:(j + 1) * tq, head_lanes(h)].astype(F32).T
                vt_ref[h, j, 0:dv, :] = vt.astype(BF16)
                vt_ref[h, j, dv:dv + ATTN_ONES_ROWS, :] = jnp.ones((ATTN_ONES_ROWS, tq), BF16)

    lane = lax.broadcasted_iota(jnp.int32, (tq, dv), 1)
    for h in range(heads):
        q = q_ref[0, :, head_lanes(h)]
        zero = jnp.zeros_like(q)
        qm_ref[h, 0] = jnp.where(lane < HEAD_QK_DIM, q, zero)
        qm_ref[h, 1] = jnp.where(lane >= HEAD_QK_DIM, q, zero)
    acc_ref[...] = jnp.zeros_like(acc_ref)
    m_ref[...] = jnp.full(m_ref.shape, -jnp.inf, F32)
    nt = (((1,), (1,)), ((), ()))

    def scores_into(s_ref, j, h):
        k = k_ref[0, pl.ds(pl.multiple_of(j * tq, tq), tq), head_lanes(h)]
        for mi in range(2):
            s_ref[mi] = lax.dot_general(k, qm_ref[h, mi], nt, preferred_element_type=F32)

    def softmax_pv(s_ref, j, h, masked):
        vt = vt_ref[h, j]
        if masked:
            key_chunk = lax.broadcasted_iota(jnp.int32, (tq, tq), 0) // CHUNK
            qry_chunk = lax.broadcasted_iota(jnp.int32, (tq, tq), 1) // CHUNK
            visible = key_chunk <= qry_chunk
            read = lambda mi: jnp.where(visible, s_ref[mi], -jnp.inf)
        else:
            read = lambda mi: s_ref[mi]
        for mi in range(2):
            m_old = m_ref[h, mi, 0:1, :]
            m_new = jnp.maximum(m_old, jnp.max(read(mi), axis=0, keepdims=True))
            alpha = jnp.exp2(m_old - m_new)
            p = jnp.exp2(read(mi) - m_new).astype(BF16)
            pv = jnp.dot(vt, p, preferred_element_type=F32)
            acc_ref[h, mi] = alpha * acc_ref[h, mi] + pv
            m_ref[h, mi] = jnp.broadcast_to(m_new, m_ref.shape[2:])

    scores_into(s_ref.at[0], 0, 0)

    def one_tile(j, masked):
        for h in range(heads):
            if h + 1 < heads:
                scores_into(s_ref.at[h + 1], j, h + 1)
            elif not masked:
                scores_into(s_ref.at[0], j + 1, 0)
            softmax_pv(s_ref.at[h], j, h, masked)

    def two_tiles(t, carry):
        one_tile(2 * t, False)
        one_tile(2 * t + 1, False)
        return carry

    lax.fori_loop(0, lax.shift_right_logical(qi, 1), two_tiles, 0)

    @pl.when(lax.bitwise_and(qi, 1) == 1)
    def _():
        one_tile(qi - 1, False)

    one_tile(qi, True)

    lam = lam_ref[0, 0]
    for h in range(heads):
        o1 = acc_ref[h, 0, 0:dv, :] * (1.0 / acc_ref[h, 0, dv:dv + 1, :])
        o2 = acc_ref[h, 1, 0:dv, :] * (lam / acc_ref[h, 1, dv:dv + 1, :])
        a = o1 - o2
        ms = jnp.mean(a * a, axis=0, keepdims=True)
        y = ((a * lax.rsqrt(ms + EPS)) * g_ref[...]) * out_scale
        o_ref[0, :, head_lanes(h)] = y.T.astype(o_ref.dtype)


def _diff_attention(qkv, lam, subln_g_col, cast_weights=(), *, batch, seq, out_scale,
                    tq=ATTN_TILE, heads=ATTN_HEADS_PER_STEP):
    assert ATTN_HEADS % heads == 0
    groups = ATTN_HEADS // heads
    n_tiles = seq // tq
    acc_rows = HEAD_V_DIM + ATTN_ONES_ROWS
    width = heads * HEAD_V_DIM
    n_steps = batch * groups * n_tiles
    step = lambda b, g, i: (b * groups + g) * n_tiles + i
    cast_in_specs, cast_out_specs, cast_shapes = [], [], []
    for w, col0, ncols in cast_weights:
        assert w.shape[0] % (n_steps * BF16_ROW_TILE) == 0 and col0 % ncols == 0
        rows = w.shape[0] // n_steps
        cast_in_specs.append(pl.BlockSpec(
            (rows, ncols), lambda b, g, i, cb=col0 // ncols: (step(b, g, i), cb)))
        cast_out_specs.append(pl.BlockSpec((rows, ncols), lambda b, g, i: (step(b, g, i), 0)))
        cast_shapes.append(jax.ShapeDtypeStruct((w.shape[0], ncols), BF16))
    return pl.pallas_call(
        functools.partial(_attn_kernel, tq=tq, n_tiles=n_tiles, heads=heads, out_scale=out_scale,
                          n_cast=len(cast_weights)),
        grid=(batch, groups, n_tiles),
        in_specs=[
            pl.BlockSpec(memory_space=pltpu.SMEM),
            pl.BlockSpec((HEAD_V_DIM, 1), lambda b, g, i: (0, 0)),
            pl.BlockSpec((1, tq, width), lambda b, g, i: (b, i, g)),
            pl.BlockSpec((1, seq, width), lambda b, g, i: (b, 0, groups + g)),
            pl.BlockSpec((1, seq, width), lambda b, g, i: (b, 0, 2 * groups + g)),
        ] + cast_in_specs,
        out_specs=[pl.BlockSpec((1, tq, width), lambda b, g, i: (b, i, g))] + cast_out_specs,
        out_shape=[jax.ShapeDtypeStruct((batch, seq, ATTN_HEADS * HEAD_V_DIM), BF16)]
        + cast_shapes,
        scratch_shapes=[pltpu.VMEM((heads, n_tiles, acc_rows, tq), BF16),
                        pltpu.VMEM((heads, 2, tq, HEAD_V_DIM), BF16),
                        pltpu.VMEM((heads, 2, acc_rows, tq), F32),
                        pltpu.VMEM((heads, 2, SUBLANES, tq), F32),
                        pltpu.VMEM((heads, 2, tq, tq), F32)],
        compiler_params=_params("arbitrary", "arbitrary", "arbitrary"),
        name="diff_attention",
    )(lam, subln_g_col, qkv, qkv, qkv, *[w for w, _, _ in cast_weights])


def _causal_conv(x, tail, cw, cb):
    ts = x.shape[0]
    xp = jnp.concatenate([tail, x], axis=0)
    xc = cb
    for j in range(CONV_WIDTH):
        off = SUBLANES - (CONV_WIDTH - 1) + j
        xc = xc + cw[j:j + 1, :] * xp[off:off + ts]
    return xc


def _block_gate_logits(xb, w_ref, b_ref):
    blocks = [jnp.dot(xb[:, n * LRU_BLOCK_DIM:(n + 1) * LRU_BLOCK_DIM], w_ref[n],
                      preferred_element_type=F32) + b_ref[n]
              for n in range(xb.shape[1] // LRU_BLOCK_DIM)]
    return jnp.concatenate(blocks, axis=1)


def _lru_scan(xc, r_logit, i_logit, gate, h0, log_sig):
    ts, c = xc.shape
    r = jax.nn.sigmoid(r_logit)
    i = jax.nn.sigmoid(i_logit)
    log_a = LRU_C * r * log_sig
    a = jnp.exp(log_a)
    b = jnp.sqrt(-jnp.tanh(log_a) * (1.0 + a * a)) * (i * xc)

    groups = ts // SUBLANES
    a = a.reshape(groups, SUBLANES, c)
    b = b.reshape(groups, SUBLANES, c)
    frame = lax.broadcasted_iota(jnp.int32, (1, SUBLANES, c), 1)
    d = 1
    while d < SUBLANES:
        keep = frame >= d
        b = b + a * jnp.where(keep, pltpu.roll(b, d, axis=1), 0.0)
        a = a * jnp.where(keep, pltpu.roll(a, d, axis=1), 1.0)
        d *= 2
    carry = h0
    hs = []
    for gi in range(groups):
        hg = b[gi] + a[gi] * carry
        carry = hg[SUBLANES - 1:SUBLANES, :]
        hs.append(hg)
    return jnp.concatenate(hs, axis=0) * jax.nn.gelu(gate, approximate=True), carry


def _rec_branch_kernel(h_ref, rstd_ref, wx_ref, wg_ref, cw_ref, cb_ref, wr_ref, br_ref, wi_ref,
                       bi_ref, lam_ref, o_ref, xr_ref, gr_ref, xc_ref, rl_ref, il_ref,
                       tail_ref, state_ref, *, nj, tiles_per_seq):
    s = pl.program_id(0)
    prev = jnp.maximum(s - 1, 0)
    ip = prev // nj
    jp = lax.rem(prev, nj)
    ts = xr_ref.shape[0]

    @pl.when(s == 0)
    def _():
        xr_ref[...] = jnp.zeros_like(xr_ref)
        gr_ref[...] = jnp.zeros_like(gr_ref)
        tail_ref[...] = jnp.zeros_like(tail_ref)
        state_ref[...] = jnp.zeros_like(state_ref)

    seq_start = lax.rem(ip, tiles_per_seq) == 0
    tail = jnp.where(seq_start, 0.0, tail_ref[jp])
    carry = jnp.where(seq_start, 0.0, state_ref[jp, 0:1, :])
    cw, cb = cw_ref[...], cb_ref[...]
    slabs = [(r0, r0 + LRU_ROW_SLAB) for r0 in range(0, ts, LRU_ROW_SLAB)]

    for r0, r1 in slabs:
        before = tail if r0 == 0 else xr_ref[r0 - SUBLANES:r0, :]
        xc_ref[r0:r1, :] = _causal_conv(xr_ref[r0:r1, :], before, cw, cb)
    tail_ref[jp] = xr_ref[ts - SUBLANES:ts, :]

    xr_ref[...] = jnp.dot(h_ref[...], wx_ref[...], preferred_element_type=F32) * rstd_ref[...]
    xb = xc_ref[...].astype(BF16)
    rl_ref[...] = _block_gate_logits(xb, wr_ref, br_ref)
    il_ref[...] = _block_gate_logits(xb, wi_ref, bi_ref)

    lam = lam_ref[...]
    log_sig = jnp.minimum(lam, 0.0) - jnp.log1p(jnp.exp(-jnp.abs(lam)))
    for r0, r1 in slabs:
        rec, carry = _lru_scan(xc_ref[r0:r1, :], rl_ref[r0:r1, :], il_ref[r0:r1, :],
                               gr_ref[r0:r1, :], carry, log_sig)
        o_ref[r0:r1, :] = rec.astype(o_ref.dtype)
    state_ref[jp] = jnp.broadcast_to(carry, state_ref.shape[1:])
    gr_ref[...] = jnp.dot(h_ref[...], wg_ref[...], preferred_element_type=F32) * rstd_ref[...]


def _rec_branch(h, rstd, w_x, w_g, conv_w, conv_b, w_r, b_r, w_i, b_i, lam, *, seq,
                tm=LRU_TILE, tc=LRU_CHANNEL_TILE):
    m, k = h.shape
    d_lru = LRU_BLOCKS * LRU_BLOCK_DIM
    assert seq % tm == 0 and d_lru % tc == 0 and w_x.shape == w_g.shape == (k, d_lru)
    nj = d_lru // tc
    n_tiles = (m // tm) * nj
    nblk = tc // LRU_BLOCK_DIM
    this_i = lambda s: jnp.minimum(s, n_tiles - 1) // nj
    this_j = lambda s: lax.rem(jnp.minimum(s, n_tiles - 1), nj)
    prev_i = lambda s: jnp.maximum(s - 1, 0) // nj
    prev_j = lambda s: lax.rem(jnp.maximum(s - 1, 0), nj)
    vec = lambda s: (0, prev_j(s))
    blk = lambda s: (prev_j(s), 0, 0)
    return pl.pallas_call(
        functools.partial(_rec_branch_kernel, nj=nj, tiles_per_seq=seq // tm),
        grid=(n_tiles + 1,),
        in_specs=[
            pl.BlockSpec((tm, k), lambda s: (this_i(s), 0)),
            pl.BlockSpec((tm, 1), lambda s: (this_i(s), 0)),
            pl.BlockSpec((k, tc), lambda s: (0, this_j(s))),
            pl.BlockSpec((k, tc), lambda s: (0, this_j(s))),
            pl.BlockSpec((CONV_WIDTH, tc), vec),
            pl.BlockSpec((1, tc), vec),
            pl.BlockSpec((nblk, LRU_BLOCK_DIM, LRU_BLOCK_DIM), blk),
            pl.BlockSpec((nblk, 1, LRU_BLOCK_DIM), blk),
            pl.BlockSpec((nblk, LRU_BLOCK_DIM, LRU_BLOCK_DIM), blk),
            pl.BlockSpec((nblk, 1, LRU_BLOCK_DIM), blk),
            pl.BlockSpec((1, tc), vec),
        ],
        out_specs=pl.BlockSpec((tm, tc), lambda s: (prev_i(s), prev_j(s))),
        out_shape=jax.ShapeDtypeStruct((m, d_lru), BF16),
        scratch_shapes=[pltpu.VMEM((tm, tc), F32), pltpu.VMEM((tm, tc), F32),
                        pltpu.VMEM((tm, tc), F32),
                        pltpu.VMEM((tm, tc), F32), pltpu.VMEM((tm, tc), F32),
                        pltpu.VMEM((nj, SUBLANES, tc), F32),
                        pltpu.VMEM((nj, SUBLANES, tc), F32)],
        compiler_params=_params("arbitrary"),
        name="rec_branch",
    )(h, rstd, w_x, w_g, conv_w, conv_b, w_r, b_r, w_i, b_i, lam)


def _out_proj_kernel(a_ref, r_ref, wa_ref, wr_ref, x_ref, g_ref, o_ref, xg_ref, rstd_ref, ss_ref,
                     *, n):
    j = pl.program_id(1)
    acc = jnp.dot(a_ref[...], wa_ref[...], preferred_element_type=F32) + x_ref[...]
    x1 = jnp.dot(r_ref[...], wr_ref[...], preferred_element_type=F32) + acc
    o_ref[...] = x1
    xg_ref[...] = (x1 * g_ref[...]).astype(xg_ref.dtype)
    ss = jnp.sum(x1 * x1, axis=-1, keepdims=True)

    @pl.when(j == 0)
    def _():
        ss_ref[...] = ss

    @pl.when(j > 0)
    def _():
        ss_ref[...] += ss

    @pl.when(j == pl.num_programs(1) - 1)
    def _():
        rstd_ref[...] = lax.rsqrt(ss_ref[...] * (1.0 / n) + EPS)


def _out_proj(attn, rec, w_out, x, g_next, *, tm, tn):
    m, ka = attn.shape
    kr = rec.shape[1]
    n = w_out.shape[1]
    assert ka == kr
    tile = lambda i, j: (i, j)
    return pl.pallas_call(
        functools.partial(_out_proj_kernel, n=n),
        grid=(m // tm, n // tn),
        in_specs=[
            pl.BlockSpec((tm, ka), lambda i, j: (i, 0)),
            pl.BlockSpec((tm, kr), lambda i, j: (i, 0)),
            pl.BlockSpec((ka, tn), lambda i, j: (0, j)),
            pl.BlockSpec((kr, tn), lambda i, j: (1, j)),
            pl.BlockSpec((tm, tn), tile),
            pl.BlockSpec((1, tn), lambda i, j: (0, j)),
        ],
        out_specs=[pl.BlockSpec((tm, tn), tile), pl.BlockSpec((tm, tn), tile),
                   pl.BlockSpec((tm, 1), lambda i, j: (i, 0))],
        out_shape=[jax.ShapeDtypeStruct((m, n), F32), jax.ShapeDtypeStruct((m, n), BF16),
                   jax.ShapeDtypeStruct((m, 1), F32)],
        scratch_shapes=[pltpu.VMEM((tm, 1), F32)],
        compiler_params=_params("arbitrary", "arbitrary"),
        name="out_proj",
    )(attn, rec, w_out, w_out, x, g_next)


def _rowscaled_matmul_kernel(xg_ref, rstd_ref, w_ref, o_ref, *, relu2, scaled_tiles, scale):
    acc = jnp.dot(xg_ref[...], w_ref[...], preferred_element_type=F32) * rstd_ref[...]
    if relu2:
        acc = jnp.square(jnp.maximum(acc, 0.0))
    if scaled_tiles:
        acc = acc * jnp.where(pl.program_id(1) < scaled_tiles, scale, 1.0).astype(F32)
    o_ref[...] = acc.astype(o_ref.dtype)


def _rowscaled_matmul(xg, rstd, w, *, out_dtype, tm, tn, name, relu2=False, scaled_cols=0,
                      scale=1.0):
    m, k = xg.shape
    n = w.shape[1]
    assert scaled_cols % tn == 0
    return pl.pallas_call(
        functools.partial(_rowscaled_matmul_kernel, relu2=relu2, scaled_tiles=scaled_cols // tn,
                          scale=scale),
        grid=(m // tm, n // tn),
        in_specs=[
            pl.BlockSpec((tm, k), lambda i, j: (i, 0)),
            pl.BlockSpec((tm, 1), lambda i, j: (i, 0)),
            pl.BlockSpec((k, tn), lambda i, j: (0, j)),
        ],
        out_specs=pl.BlockSpec((tm, tn), lambda i, j: (i, j)),
        out_shape=jax.ShapeDtypeStruct((m, n), out_dtype),
        compiler_params=_params("arbitrary", "arbitrary"),
        name=name,
    )(xg, rstd, w)


def _norm_prep_kernel(x_ref, g_ref, xg_ref, rstd_ref):
    for r0 in range(0, x_ref.shape[0], NORM_SLAB_ROWS):
        rows = slice(r0, r0 + NORM_SLAB_ROWS)
        x = x_ref[rows, :]
        xg_ref[rows, :] = (x * g_ref[...]).astype(xg_ref.dtype)
        rstd_ref[rows, :] = lax.rsqrt(jnp.mean(x * x, axis=-1, keepdims=True) + EPS)


def _norm_prep(x, g, *, tm):
    m, k = x.shape
    return pl.pallas_call(
        _norm_prep_kernel,
        grid=(m // tm,),
        in_specs=[pl.BlockSpec((tm, k), lambda i: (i, 0)), pl.BlockSpec((1, k), lambda i: (0, 0))],
        out_specs=[pl.BlockSpec((tm, k), lambda i: (i, 0)), pl.BlockSpec((tm, 1), lambda i: (i, 0))],
        out_shape=[jax.ShapeDtypeStruct((m, k), BF16), jax.ShapeDtypeStruct((m, 1), F32)],
        compiler_params=_params("arbitrary"),
        name="norm_prep",
    )(x, g)


def _down_norm_kernel(u_ref, w_ref, x_ref, g_ref, o_ref):
    k = pl.program_id(1)
    half = x_ref.shape[1]

    @pl.when(k == 0)
    def _():
        u = u_ref[...]
        o_ref[:, :half] = jnp.dot(u, w_ref[:, :half], preferred_element_type=F32) + x_ref[...]
        o_ref[:, half:] = jnp.dot(u, w_ref[:, half:], preferred_element_type=F32)

    @pl.when(k > 0)
    def _():
        o_ref[...] = jnp.dot(u_ref[...], w_ref[...], preferred_element_type=F32) + o_ref[...]

    @pl.when(k == 1)
    def _():
        o_ref[:, half:] += x_ref[...]

    @pl.when(k == pl.num_programs(1) - 1)
    def _():
        for r0 in range(0, o_ref.shape[0], NORM_SLAB_ROWS):
            rows = slice(r0, r0 + NORM_SLAB_ROWS)
            o_ref[rows, :] = _rmsnorm_rows(o_ref[rows, :], g_ref[...])


def _down_norm(u, w, x, g, *, tm, tk):
    m, kk = u.shape
    n = w.shape[1]
    assert kk // tk >= 2
    return pl.pallas_call(
        _down_norm_kernel,
        grid=(m // tm, kk // tk),
        in_specs=[
            pl.BlockSpec((tm, tk), lambda i, k: (i, k)),
            pl.BlockSpec((tk, n), lambda i, k: (k, 0)),
            pl.BlockSpec((tm, n // 2), lambda i, k: (i, jnp.minimum(k, 1))),
            pl.BlockSpec((1, n), lambda i, k: (0, 0)),
        ],
        out_specs=pl.BlockSpec((tm, n), lambda i, k: (i, 0), pipeline_mode=pl.Buffered(1)),
        out_shape=jax.ShapeDtypeStruct((m, n), F32),
        compiler_params=_params("arbitrary", "arbitrary"),
        name="down_norm",
    )(u, w, x, g)


def kernel(x, norm_mix_g, w_in, conv_w, conv_b, w_rgate, b_rgate, w_igate, b_igate, lru_lambda, lambda_q1, lambda_k1, lambda_q2, lambda_k2, subln_g, w_out, norm_mlp_g, w_mlp_up, w_mlp_down, norm_final_g):
    batch, seq, d_model = x.shape
    depth = w_in.shape[0]
    d_attn = ATTN_HEADS * HEAD_V_DIM
    d_lru = LRU_BLOCKS * LRU_BLOCK_DIM
    m = batch * seq
    row = lambda v: v.reshape(1, -1)

    xf = x.reshape(m, d_model)
    for l in range(depth):
        lambda_init = 0.8 - 0.6 * math.exp(-0.3 * l)
        xg_mix, rstd_mix = _norm_prep(xf, row(norm_mix_g[l]), tm=NORM_PREP_ROWS)
        w_qkv = w_in[l][:, :3 * d_attn].astype(BF16)
        qkv = _rowscaled_matmul(xg_mix, rstd_mix, w_qkv, out_dtype=BF16, scaled_cols=d_attn,
                                scale=ATTN_Q_PRESCALE, name="qkv_proj", **QKV_PROJ_TILES)

        lam = _diff_lambda(row(lambda_q1[l]), row(lambda_k1[l]), row(lambda_q2[l]),
                           row(lambda_k2[l]), lambda_init)
        full = lambda w: (w, 0, w.shape[1])
        attn, w_x_l, w_g_l, w_out_l, w_up_l, w_down_l = _diff_attention(
            qkv.reshape(batch, seq, 3 * d_attn), lam, subln_g[l].reshape(HEAD_V_DIM, 1),
            ((w_in[l], 3 * d_attn, d_lru), (w_in[l], 3 * d_attn + d_lru, d_lru),
             full(w_out[l]), full(w_mlp_up[l]), full(w_mlp_down[l])),
            batch=batch, seq=seq, out_scale=1.0 - lambda_init)

        rec = _rec_branch(
            xg_mix, rstd_mix, w_x_l, w_g_l, conv_w[l], row(conv_b[l]),
            w_rgate[l].astype(BF16), b_rgate[l].reshape(LRU_BLOCKS, 1, LRU_BLOCK_DIM),
            w_igate[l].astype(BF16), b_igate[l].reshape(LRU_BLOCKS, 1, LRU_BLOCK_DIM),
            row(lru_lambda[l]), seq=seq)

        xf, xg, rstd = _out_proj(attn.reshape(m, d_attn), rec, w_out_l, xf, row(norm_mlp_g[l]),
                                 **OUT_PROJ_TILES)

        u = _rowscaled_matmul(xg, rstd, w_up_l, out_dtype=BF16, relu2=True, name="up_proj",
                              **UP_PROJ_TILES)
        last = l == depth - 1
        assert last, "DEPTH > 1 needs an un-normalised down-projection variant"
        xf = _down_norm(u, w_down_l, xf, row(norm_final_g), **DOWN_PROJ_TILES)
    return xf.reshape(batch, seq, d_model)
```

```python
import functools
import math

import jax
import jax.numpy as jnp
from jax import lax
from jax.experimental import pallas as pl
from jax.experimental.pallas import tpu as pltpu

F32 = jnp.float32
BF16 = jnp.bfloat16

EPS = 1e-6
CHUNK = 64
ATTN_HEADS = 16
HEAD_V_DIM = 128
HEAD_QK_DIM = 64
LRU_BLOCKS = 16
LRU_BLOCK_DIM = 128
CONV_WIDTH = 4
LRU_C = 8.0

V7X_VMEM_LIMIT_BYTES = 60000 * 1024
SUBLANES = 8
BF16_ROW_TILE = 16

NORM_PREP_ROWS = 512
QKV_PROJ_TILES = dict(tm=1024, tn=1024)
OUT_PROJ_TILES = dict(tm=1024, tn=512)
UP_PROJ_TILES = dict(tm=1024, tn=1024)
DOWN_PROJ_TILES = dict(tm=512, tk=2048)
DOWN_RESIDUAL_PIECES = 4
ATTN_TILE = 512
ATTN_HEADS_PER_STEP = 4
LRU_TILE = 512
LRU_CHANNEL_TILE = 256
LRU_ROW_SLAB = 32
NORM_SLAB_ROWS = 32


def _params(*semantics):
    return pltpu.CompilerParams(dimension_semantics=semantics,
                                vmem_limit_bytes=V7X_VMEM_LIMIT_BYTES)


def _rmsnorm_rows(x, g):
    ms = jnp.mean(x * x, axis=-1, keepdims=True)
    return (x * lax.rsqrt(ms + EPS)) * g


def _lambda_kernel(q1_ref, k1_ref, q2_ref, k2_ref, o_ref, *, lambda_init):
    d1 = jnp.sum(q1_ref[...] * k1_ref[...], axis=-1, keepdims=True)
    d2 = jnp.sum(q2_ref[...] * k2_ref[...], axis=-1, keepdims=True)
    o_ref[...] = jnp.exp(d1) - jnp.exp(d2) + lambda_init


def _diff_lambda(q1, k1, q2, k2, lambda_init):
    return pl.pallas_call(
        functools.partial(_lambda_kernel, lambda_init=lambda_init),
        out_shape=jax.ShapeDtypeStruct((1, 1), F32),
        name="diff_lambda",
    )(q1, k1, q2, k2)


ATTN_ONES_ROWS = 16
ATTN_Q_PRESCALE = HEAD_QK_DIM ** -0.5 * math.log2(math.e)


def _attn_kernel(lam_ref, g_ref, q_ref, k_ref, v_ref, *refs, tq, n_tiles, heads, out_scale,
                 n_cast):
    cast_in, o_ref, cast_out = refs[:n_cast], refs[n_cast], refs[n_cast + 1:2 * n_cast + 1]
    vt_ref, qm_ref, acc_ref, m_ref, s_ref = refs[2 * n_cast + 1:]
    qi = pl.program_id(2)
    dv = HEAD_V_DIM
    head_lanes = lambda h: slice(h * dv, (h + 1) * dv)

    for w_ref, wb_ref in zip(cast_in, cast_out):
        wb_ref[...] = w_ref[...].astype(BF16)

    @pl.when(qi == 0)
    def _():
        for h in range(heads):
            for j in range(n_tiles):
                vt = v_ref[0, j * tq:(j + 1) * tq, head_lanes(h)].astype(F32).T
                vt_ref[h, j, 0:dv, :] = vt.astype(BF16)
                vt_ref[h, j, dv:dv + ATTN_ONES_ROWS, :] = jnp.ones((ATTN_ONES_ROWS, tq), BF16)

    lane = lax.broadcasted_iota(jnp.int32, (tq, dv), 1)
    for h in range(heads):
        q = q_ref[0, :, head_lanes(h)]
        zero = jnp.zeros_like(q)
        qm_ref[h, 0] = jnp.where(lane < HEAD_QK_DIM, q, zero)
        qm_ref[h, 1] = jnp.where(lane >= HEAD_QK_DIM, q, zero)
    acc_ref[...] = jnp.zeros_like(acc_ref)
    m_ref[...] = jnp.full(m_ref.shape, -jnp.inf, F32)
    nt = (((1,), (1,)), ((), ()))

    def scores_into(s_ref, j, h):
        k = k_ref[0, pl.ds(pl.multiple_of(j * tq, tq), tq), head_lanes(h)]
        for mi in range(2):
            s_ref[mi] = lax.dot_general(k, qm_ref[h, mi], nt, preferred_element_type=F32)

    def softmax_pv(s_ref, j, h, masked):
        vt = vt_ref[h, j]
        if masked:
            key_chunk = lax.broadcasted_iota(jnp.int32, (tq, tq), 0) // CHUNK
            qry_chunk = lax.broadcasted_iota(jnp.int32, (tq, tq), 1) // CHUNK
            visible = key_chunk <= qry_chunk
            read = lambda mi: jnp.where(visible, s_ref[mi], -jnp.inf)
        else:
            read = lambda mi: s_ref[mi]
        for mi in range(2):
            m_old = m_ref[h, mi, 0:1, :]
            m_new = jnp.maximum(m_old, jnp.max(read(mi), axis=0, keepdims=True))
            alpha = jnp.exp2(m_old - m_new)
            p = jnp.exp2(read(mi) - m_new).astype(BF16)
            pv = jnp.dot(vt, p, preferred_element_type=F32)
            acc_ref[h, mi] = alpha * acc_ref[h, mi] + pv
            m_ref[h, mi] = jnp.broadcast_to(m_new, m_ref.shape[2:])

    scores_into(s_ref.at[0], 0, 0)

    def one_tile(j, masked):
        for h in range(heads):
            if h + 1 < heads:
                scores_into(s_ref.at[h + 1], j, h + 1)
            elif not masked:
                scores_into(s_ref.at[0], j + 1, 0)
            softmax_pv(s_ref.at[h], j, h, masked)

    def two_tiles(t, carry):
        one_tile(2 * t, False)
        one_tile(2 * t + 1, False)
        return carry

    lax.fori_loop(0, lax.shift_right_logical(qi, 1), two_tiles, 0)

    @pl.when(lax.bitwise_and(qi, 1) == 1)
    def _():
        one_tile(qi - 1, False)

    one_tile(qi, True)

    lam = lam_ref[0, 0]
    for h in range(heads):
        o1 = acc_ref[h, 0, 0:dv, :] * (1.0 / acc_ref[h, 0, dv:dv + 1, :])
        o2 = acc_ref[h, 1, 0:dv, :] * (lam / acc_ref[h, 1, dv:dv + 1, :])
        a = o1 - o2
        ms = jnp.mean(a * a, axis=0, keepdims=True)
        y = ((a * lax.rsqrt(ms + EPS)) * g_ref[...]) * out_scale
        o_ref[0, :, head_lanes(h)] = y.T.astype(o_ref.dtype)


def _diff_attention(qkv, lam, subln_g_col, cast_weights=(), *, batch, seq, out_scale,
                    tq=ATTN_TILE, heads=ATTN_HEADS_PER_STEP):
    assert ATTN_HEADS % heads == 0
    groups = ATTN_HEADS // heads
    n_tiles = seq // tq
    acc_rows = HEAD_V_DIM + ATTN_ONES_ROWS
    width = heads * HEAD_V_DIM
    n_steps = batch * groups * n_tiles
    step = lambda b, g, i: (b * groups + g) * n_tiles + i
    cast_in_specs, cast_out_specs, cast_shapes = [], [], []
    for w, col0, ncols in cast_weights:
        assert w.shape[0] % (n_steps * BF16_ROW_TILE) == 0 and col0 % ncols == 0
        rows = w.shape[0] // n_steps
        cast_in_specs.append(pl.BlockSpec(
            (rows, ncols), lambda b, g, i, cb=col0 // ncols: (step(b, g, i), cb)))
        cast_out_specs.append(pl.BlockSpec((rows, ncols), lambda b, g, i: (step(b, g, i), 0)))
        cast_shapes.append(jax.ShapeDtypeStruct((w.shape[0], ncols), BF16))
    return pl.pallas_call(
        functools.partial(_attn_kernel, tq=tq, n_tiles=n_tiles, heads=heads, out_scale=out_scale,
                          n_cast=len(cast_weights)),
        grid=(batch, groups, n_tiles),
        in_specs=[
            pl.BlockSpec(memory_space=pltpu.SMEM),
            pl.BlockSpec((HEAD_V_DIM, 1), lambda b, g, i: (0, 0)),
            pl.BlockSpec((1, tq, width), lambda b, g, i: (b, i, g)),
            pl.BlockSpec((1, seq, width), lambda b, g, i: (b, 0, groups + g)),
            pl.BlockSpec((1, seq, width), lambda b, g, i: (b, 0, 2 * groups + g)),
        ] + cast_in_specs,
        out_specs=[pl.BlockSpec((1, tq, width), lambda b, g, i: (b, i, g))] + cast_out_specs,
        out_shape=[jax.ShapeDtypeStruct((batch, seq, ATTN_HEADS * HEAD_V_DIM), BF16)]
        + cast_shapes,
        scratch_shapes=[pltpu.VMEM((heads, n_tiles, acc_rows, tq), BF16),
                        pltpu.VMEM((heads, 2, tq, HEAD_V_DIM), BF16),
                        pltpu.VMEM((heads, 2, acc_rows, tq), F32),
                        pltpu.VMEM((heads, 2, SUBLANES, tq), F32),
                        pltpu.VMEM((heads, 2, tq, tq), F32)],
        compiler_params=_params("arbitrary", "arbitrary", "arbitrary"),
        name="diff_attention",
    )(lam, subln_g_col, qkv, qkv, qkv, *[w for w, _, _ in cast_weights])


def _causal_conv(x, tail, cw, cb):
    ts = x.shape[0]
    xp = jnp.concatenate([tail, x], axis=0)
    xc = cb
    for j in range(CONV_WIDTH):
        off = SUBLANES - (CONV_WIDTH - 1) + j
        xc = xc + cw[j:j + 1, :] * xp[off:off + ts]
    return xc


def _block_gate_logits(xb, w_ref, b_ref):
    blocks = [jnp.dot(xb[:, n * LRU_BLOCK_DIM:(n + 1) * LRU_BLOCK_DIM], w_ref[n],
                      preferred_element_type=F32) + b_ref[n]
              for n in range(xb.shape[1] // LRU_BLOCK_DIM)]
    return jnp.concatenate(blocks, axis=1)


def _lru_scan(xc, r_logit, i_logit, gate, h0, log_sig):
    ts, c = xc.shape
    r = jax.nn.sigmoid(r_logit)
    i = jax.nn.sigmoid(i_logit)
    log_a = LRU_C * r * log_sig
    a = jnp.exp(log_a)
    b = jnp.sqrt(-jnp.tanh(log_a) * (1.0 + a * a)) * (i * xc)

    groups = ts // SUBLANES
    a = a.reshape(groups, SUBLANES, c)
    b = b.reshape(groups, SUBLANES, c)
    frame = lax.broadcasted_iota(jnp.int32, (1, SUBLANES, c), 1)
    d = 1
    while d < SUBLANES:
        keep = frame >= d
        b = b + a * jnp.where(keep, pltpu.roll(b, d, axis=1), 0.0)
        a = a * jnp.where(keep, pltpu.roll(a, d, axis=1), 1.0)
        d *= 2
    carry = h0
    hs = []
    for gi in range(groups):
        hg = b[gi] + a[gi] * carry
        carry = hg[SUBLANES - 1:SUBLANES, :]
        hs.append(hg)
    return jnp.concatenate(hs, axis=0) * jax.nn.gelu(gate, approximate=True), carry


def _rec_branch_kernel(h_ref, rstd_ref, wx_ref, wg_ref, cw_ref, cb_ref, wr_ref, br_ref, wi_ref,
                       bi_ref, lam_ref, o_ref, xr_ref, gr_ref, xc_ref, rl_ref, il_ref,
                       tail_ref, state_ref, *, nj, tiles_per_seq):
    s = pl.program_id(0)
    prev = jnp.maximum(s - 1, 0)
    ip = prev // nj
    jp = lax.rem(prev, nj)
    ts = xr_ref.shape[0]

    @pl.when(s == 0)
    def _():
        xr_ref[...] = jnp.zeros_like(xr_ref)
        gr_ref[...] = jnp.zeros_like(gr_ref)
        tail_ref[...] = jnp.zeros_like(tail_ref)
        state_ref[...] = jnp.zeros_like(state_ref)

    seq_start = lax.rem(ip, tiles_per_seq) == 0
    tail = jnp.where(seq_start, 0.0, tail_ref[jp])
    carry = jnp.where(seq_start, 0.0, state_ref[jp, 0:1, :])
    cw, cb = cw_ref[...], cb_ref[...]
    slabs = [(r0, r0 + LRU_ROW_SLAB) for r0 in range(0, ts, LRU_ROW_SLAB)]

    for r0, r1 in slabs:
        before = tail if r0 == 0 else xr_ref[r0 - SUBLANES:r0, :]
        xc_ref[r0:r1, :] = _causal_conv(xr_ref[r0:r1, :], before, cw, cb)
    tail_ref[jp] = xr_ref[ts - SUBLANES:ts, :]

    xr_ref[...] = jnp.dot(h_ref[...], wx_ref[...], preferred_element_type=F32) * rstd_ref[...]
    xb = xc_ref[...].astype(BF16)
    rl_ref[...] = _block_gate_logits(xb, wr_ref, br_ref)
    il_ref[...] = _block_gate_logits(xb, wi_ref, bi_ref)

    lam = lam_ref[...]
    log_sig = jnp.minimum(lam, 0.0) - jnp.log1p(jnp.exp(-jnp.abs(lam)))
    for r0, r1 in slabs:
        rec, carry = _lru_scan(xc_ref[r0:r1, :], rl_ref[r0:r1, :], il_ref[r0:r1, :],
                               gr_ref[r0:r1, :], carry, log_sig)
        o_ref[r0:r1, :] = rec.astype(o_ref.dtype)
    state_ref[jp] = jnp.broadcast_to(carry, state_ref.shape[1:])
    gr_ref[...] = jnp.dot(h_ref[...], wg_ref[...], preferred_element_type=F32) * rstd_ref[...]


def _rec_branch(h, rstd, w_x, w_g, conv_w, conv_b, w_r, b_r, w_i, b_i, lam, *, seq,
                tm=LRU_TILE, tc=LRU_CHANNEL_TILE):
    m, k = h.shape
    d_lru = LRU_BLOCKS * LRU_BLOCK_DIM
    assert seq % tm == 0 and d_lru % tc == 0 and w_x.shape == w_g.shape == (k, d_lru)
    nj = d_lru // tc
    n_tiles = (m // tm) * nj
    nblk = tc // LRU_BLOCK_DIM
    this_i = lambda s: jnp.minimum(s, n_tiles - 1) // nj
    this_j = lambda s: lax.rem(jnp.minimum(s, n_tiles - 1), nj)
    prev_i = lambda s: jnp.maximum(s - 1, 0) // nj
    prev_j = lambda s: lax.rem(jnp.maximum(s - 1, 0), nj)
    vec = lambda s: (0, prev_j(s))
    blk = lambda s: (prev_j(s), 0, 0)
    return pl.pallas_call(
        functools.partial(_rec_branch_kernel, nj=nj, tiles_per_seq=seq // tm),
        grid=(n_tiles + 1,),
        in_specs=[
            pl.BlockSpec((tm, k), lambda s: (this_i(s), 0)),
            pl.BlockSpec((tm, 1), lambda s: (this_i(s), 0)),
            pl.BlockSpec((k, tc), lambda s: (0, this_j(s))),
            pl.BlockSpec((k, tc), lambda s: (0, this_j(s))),
            pl.BlockSpec((CONV_WIDTH, tc), vec),
            pl.BlockSpec((1, tc), vec),
            pl.BlockSpec((nblk, LRU_BLOCK_DIM, LRU_BLOCK_DIM), blk),
            pl.BlockSpec((nblk, 1, LRU_BLOCK_DIM), blk),
            pl.BlockSpec((nblk, LRU_BLOCK_DIM, LRU_BLOCK_DIM), blk),
            pl.BlockSpec((nblk, 1, LRU_BLOCK_DIM), blk),
            pl.BlockSpec((1, tc), vec),
        ],
        out_specs=pl.BlockSpec((tm, tc), lambda s: (prev_i(s), prev_j(s))),
        out_shape=jax.ShapeDtypeStruct((m, d_lru), BF16),
        scratch_shapes=[pltpu.VMEM((tm, tc), F32), pltpu.VMEM((tm, tc), F32),
                        pltpu.VMEM((tm, tc), F32),
                        pltpu.VMEM((tm, tc), F32), pltpu.VMEM((tm, tc), F32),
                        pltpu.VMEM((nj, SUBLANES, tc), F32),
                        pltpu.VMEM((nj, SUBLANES, tc), F32)],
        compiler_params=_params("arbitrary"),
        name="rec_branch",
    )(h, rstd, w_x, w_g, conv_w, conv_b, w_r, b_r, w_i, b_i, lam)


def _out_proj_kernel(a_ref, r_ref, wa_ref, wr_ref, x_ref, g_ref, o_ref, xg_ref, rstd_ref, ss_ref,
                     *, n):
    j = pl.program_id(1)
    acc = jnp.dot(a_ref[...], wa_ref[...], preferred_element_type=F32) + x_ref[...]
    x1 = jnp.dot(r_ref[...], wr_ref[...], preferred_element_type=F32) + acc
    o_ref[...] = x1
    xg_ref[...] = (x1 * g_ref[...]).astype(xg_ref.dtype)
    ss = jnp.sum(x1 * x1, axis=-1, keepdims=True)

    @pl.when(j == 0)
    def _():
        ss_ref[...] = ss

    @pl.when(j > 0)
    def _():
        ss_ref[...] += ss

    @pl.when(j == pl.num_programs(1) - 1)
    def _():
        rstd_ref[...] = lax.rsqrt(ss_ref[...] * (1.0 / n) + EPS)


def _out_proj(attn, rec, w_out, x, g_next, *, tm, tn):
    m, ka = attn.shape
    kr = rec.shape[1]
    n = w_out.shape[1]
    assert ka == kr
    tile = lambda i, j: (i, j)
    return pl.pallas_call(
        functools.partial(_out_proj_kernel, n=n),
        grid=(m // tm, n // tn),
        in_specs=[
            pl.BlockSpec((tm, ka), lambda i, j: (i, 0)),
            pl.BlockSpec((tm, kr), lambda i, j: (i, 0)),
            pl.BlockSpec((ka, tn), lambda i, j: (0, j)),
            pl.BlockSpec((kr, tn), lambda i, j: (1, j)),
            pl.BlockSpec((tm, tn), tile),
            pl.BlockSpec((1, tn), lambda i, j: (0, j)),
        ],
        out_specs=[pl.BlockSpec((tm, tn), tile), pl.BlockSpec((tm, tn), tile),
                   pl.BlockSpec((tm, 1), lambda i, j: (i, 0))],
        out_shape=[jax.ShapeDtypeStruct((m, n), F32), jax.ShapeDtypeStruct((m, n), BF16),
                   jax.ShapeDtypeStruct((m, 1), F32)],
        scratch_shapes=[pltpu.VMEM((tm, 1), F32)],
        compiler_params=_params("arbitrary", "arbitrary"),
        name="out_proj",
    )(attn, rec, w_out, w_out, x, g_next)


def _rowscaled_matmul_kernel(xg_ref, rstd_ref, w_ref, o_ref, *, relu2, scaled_tiles, scale):
    acc = jnp.dot(xg_ref[...], w_ref[...], preferred_element_type=F32) * rstd_ref[...]
    if relu2:
        acc = jnp.square(jnp.maximum(acc, 0.0))
    if scaled_tiles:
        acc = acc * jnp.where(pl.program_id(1) < scaled_tiles, scale, 1.0).astype(F32)
    o_ref[...] = acc.astype(o_ref.dtype)


def _rowscaled_matmul(xg, rstd, w, *, out_dtype, tm, tn, name, relu2=False, scaled_cols=0,
                      scale=1.0):
    m, k = xg.shape
    n = w.shape[1]
    assert scaled_cols % tn == 0
    return pl.pallas_call(
        functools.partial(_rowscaled_matmul_kernel, relu2=relu2, scaled_tiles=scaled_cols // tn,
                          scale=scale),
        grid=(m // tm, n // tn),
        in_specs=[
            pl.BlockSpec((tm, k), lambda i, j: (i, 0)),
            pl.BlockSpec((tm, 1), lambda i, j: (i, 0)),
            pl.BlockSpec((k, tn), lambda i, j: (0, j)),
        ],
        out_specs=pl.BlockSpec((tm, tn), lambda i, j: (i, j)),
        out_shape=jax.ShapeDtypeStruct((m, n), out_dtype),
        compiler_params=_params("arbitrary", "arbitrary"),
        name=name,
    )(xg, rstd, w)


def _norm_prep_kernel(x_ref, g_ref, xg_ref, rstd_ref):
    for r0 in range(0, x_ref.shape[0], NORM_SLAB_ROWS):
        rows = slice(r0, r0 + NORM_SLAB_ROWS)
        x = x_ref[rows, :]
        xg_ref[rows, :] = (x * g_ref[...]).astype(xg_ref.dtype)
        rstd_ref[rows, :] = lax.rsqrt(jnp.mean(x * x, axis=-1, keepdims=True) + EPS)


def _norm_prep(x, g, *, tm):
    m, k = x.shape
    return pl.pallas_call(
        _norm_prep_kernel,
        grid=(m // tm,),
        in_specs=[pl.BlockSpec((tm, k), lambda i: (i, 0)), pl.BlockSpec((1, k), lambda i: (0, 0))],
        out_specs=[pl.BlockSpec((tm, k), lambda i: (i, 0)), pl.BlockSpec((tm, 1), lambda i: (i, 0))],
        out_shape=[jax.ShapeDtypeStruct((m, k), BF16), jax.ShapeDtypeStruct((m, 1), F32)],
        compiler_params=_params("arbitrary"),
        name="norm_prep",
    )(x, g)


def _down_norm_kernel(u_ref, w_ref, x_ref, g_ref, o_ref):
    k = pl.program_id(1)
    pw = x_ref.shape[1]

    @pl.when(k == 0)
    def _():
        o_ref[:, :pw] = x_ref[...]
        o_ref[:, pw:] = jnp.zeros((o_ref.shape[0], o_ref.shape[1] - pw), F32)

    o_ref[...] = jnp.dot(u_ref[...], w_ref[...], preferred_element_type=F32) + o_ref[...]

    for p in range(1, o_ref.shape[1] // pw):
        @pl.when(k == p)
        def _(p=p):
            o_ref[:, p * pw:(p + 1) * pw] += x_ref[...]

    @pl.when(k == pl.num_programs(1) - 1)
    def _():
        for r0 in range(0, o_ref.shape[0], NORM_SLAB_ROWS):
            rows = slice(r0, r0 + NORM_SLAB_ROWS)
            o_ref[rows, :] = _rmsnorm_rows(o_ref[rows, :], g_ref[...])


def _down_norm(u, w, x, g, *, tm, tk):
    m, kk = u.shape
    n = w.shape[1]
    pieces = DOWN_RESIDUAL_PIECES
    assert kk // tk >= pieces and n % pieces == 0
    return pl.pallas_call(
        _down_norm_kernel,
        grid=(m // tm, kk // tk),
        in_specs=[
            pl.BlockSpec((tm, tk), lambda i, k: (i, k)),
            pl.BlockSpec((tk, n), lambda i, k: (k, 0)),
            pl.BlockSpec((tm, n // pieces), lambda i, k: (i, jnp.minimum(k, pieces - 1))),
            pl.BlockSpec((1, n), lambda i, k: (0, 0)),
        ],
        out_specs=pl.BlockSpec((tm, n), lambda i, k: (i, 0)),
        out_shape=jax.ShapeDtypeStruct((m, n), F32),
        compiler_params=_params("arbitrary", "arbitrary"),
        name="down_norm",
    )(u, w, x, g)


def kernel(x, norm_mix_g, w_in, conv_w, conv_b, w_rgate, b_rgate, w_igate, b_igate, lru_lambda, lambda_q1, lambda_k1, lambda_q2, lambda_k2, subln_g, w_out, norm_mlp_g, w_mlp_up, w_mlp_down, norm_final_g):
    batch, seq, d_model = x.shape
    depth = w_in.shape[0]
    d_attn = ATTN_HEADS * HEAD_V_DIM
    d_lru = LRU_BLOCKS * LRU_BLOCK_DIM
    m = batch * seq
    row = lambda v: v.reshape(1, -1)

    xf = x.reshape(m, d_model)
    for l in range(depth):
        lambda_init = 0.8 - 0.6 * math.exp(-0.3 * l)
        xg_mix, rstd_mix = _norm_prep(xf, row(norm_mix_g[l]), tm=NORM_PREP_ROWS)
        w_qkv = w_in[l][:, :3 * d_attn].astype(BF16)
        qkv = _rowscaled_matmul(xg_mix, rstd_mix, w_qkv, out_dtype=BF16, scaled_cols=d_attn,
                                scale=ATTN_Q_PRESCALE, name="qkv_proj", **QKV_PROJ_TILES)

        lam = _diff_lambda(row(lambda_q1[l]), row(lambda_k1[l]), row(lambda_q2[l]),
                           row(lambda_k2[l]), lambda_init)
        full = lambda w: (w, 0, w.shape[1])
        attn, w_x_l, w_g_l, w_out_l, w_up_l, w_down_l = _diff_attention(
            qkv.reshape(batch, seq, 3 * d_attn), lam, subln_g[l].reshape(HEAD_V_DIM, 1),
            ((w_in[l], 3 * d_attn, d_lru), (w_in[l], 3 * d_attn + d_lru, d_lru),
             full(w_out[l]), full(w_mlp_up[l]), full(w_mlp_down[l])),
            batch=batch, seq=seq, out_scale=1.0 - lambda_init)

        rec = _rec_branch(
            xg_mix, rstd_mix, w_x_l, w_g_l, conv_w[l], row(conv_b[l]),
            w_rgate[l].astype(BF16), b_rgate[l].reshape(LRU_BLOCKS, 1, LRU_BLOCK_DIM),
            w_igate[l].astype(BF16), b_igate[l].reshape(LRU_BLOCKS, 1, LRU_BLOCK_DIM),
            row(lru_lambda[l]), seq=seq)

        xf, xg, rstd = _out_proj(attn.reshape(m, d_attn), rec, w_out_l, xf, row(norm_mlp_g[l]),
                                 **OUT_PROJ_TILES)

        u = _rowscaled_matmul(xg, rstd, w_up_l, out_dtype=BF16, relu2=True, name="up_proj",
                              **UP_PROJ_TILES)
        last = l == depth - 1
        assert last, "DEPTH > 1 needs an un-normalised down-projection variant"
        xf = _down_norm(u, w_down_l, xf, row(norm_final_g), **DOWN_PROJ_TILES)
    return xf.reshape(batch, seq, d_model)
```

```python
import functools
import math

import jax
import jax.numpy as jnp
from jax import lax
from jax.experimental import pallas as pl
from jax.experimental.pallas import tpu as pltpu

F32 = jnp.float32
BF16 = jnp.bfloat16

EPS = 1e-6
CHUNK = 64
ATTN_HEADS = 16
HEAD_V_DIM = 128
HEAD_QK_DIM = 64
LRU_BLOCKS = 16
LRU_BLOCK_DIM = 128
CONV_WIDTH = 4
LRU_C = 8.0

V7X_VMEM_LIMIT_BYTES = 60000 * 1024
SUBLANES = 8
BF16_ROW_TILE = 16

NORM_PREP_ROWS = 512
QKV_PROJ_TILES = dict(tm=1024, tn=1024)
OUT_PROJ_TILES = dict(tm=1024, tn=512)
UP_PROJ_TILES = dict(tm=1024, tn=1024)
DOWN_PROJ_TILES = dict(tm=512, tk=2048)
DOWN_RESIDUAL_PIECES = 4
ATTN_TILE = 512
ATTN_HEADS_PER_STEP = 4
LRU_TILE = 512
LRU_CHANNEL_TILE = 256
LRU_ROW_SLAB = 32
NORM_SLAB_ROWS = 32


def _params(*semantics, allow_input_fusion=None):
    return pltpu.CompilerParams(dimension_semantics=semantics,
                                vmem_limit_bytes=V7X_VMEM_LIMIT_BYTES,
                                allow_input_fusion=allow_input_fusion)


def _rmsnorm_rows(x, g):
    ms = jnp.mean(x * x, axis=-1, keepdims=True)
    return (x * lax.rsqrt(ms + EPS)) * g


def _lambda_kernel(q1_ref, k1_ref, q2_ref, k2_ref, o_ref, *, lambda_init):
    d1 = jnp.sum(q1_ref[...] * k1_ref[...], axis=-1, keepdims=True)
    d2 = jnp.sum(q2_ref[...] * k2_ref[...], axis=-1, keepdims=True)
    o_ref[...] = jnp.exp(d1) - jnp.exp(d2) + lambda_init


def _diff_lambda(q1, k1, q2, k2, lambda_init):
    return pl.pallas_call(
        functools.partial(_lambda_kernel, lambda_init=lambda_init),
        out_shape=jax.ShapeDtypeStruct((1, 1), F32),
        name="diff_lambda",
    )(q1, k1, q2, k2)


ATTN_ONES_ROWS = 16
ATTN_Q_PRESCALE = HEAD_QK_DIM ** -0.5 * math.log2(math.e)


def _attn_kernel(lam_ref, g_ref, q_ref, k_ref, v_ref, *refs, tq, n_tiles, heads, out_scale,
                 n_cast):
    cast_in, o_ref, cast_out = refs[:n_cast], refs[n_cast], refs[n_cast + 1:2 * n_cast + 1]
    vt_ref, qm_ref, acc_ref, m_ref, s_ref = refs[2 * n_cast + 1:]
    qi = pl.program_id(2)
    dv = HEAD_V_DIM
    head_lanes = lambda h: slice(h * dv, (h + 1) * dv)

    for w_ref, wb_ref in zip(cast_in, cast_out):
        wb_ref[...] = w_ref[...].astype(BF16)

    @pl.when(qi == 0)
    def _():
        for h in range(heads):
            for j in range(n_tiles):
                vt = v_ref[0, j * tq:(j + 1) * tq, head_lanes(h)].astype(F32).T
                vt_ref[h, j, 0:dv, :] = vt.astype(BF16)
                vt_ref[h, j, dv:dv + ATTN_ONES_ROWS, :] = jnp.ones((ATTN_ONES_ROWS, tq), BF16)

    lane = lax.broadcasted_iota(jnp.int32, (tq, dv), 1)
    for h in range(heads):
        q = q_ref[0, :, head_lanes(h)]
        zero = jnp.zeros_like(q)
        qm_ref[h, 0] = jnp.where(lane < HEAD_QK_DIM, q, zero)
        qm_ref[h, 1] = jnp.where(lane >= HEAD_QK_DIM, q, zero)
    acc_ref[...] = jnp.zeros_like(acc_ref)
    m_ref[...] = jnp.full(m_ref.shape, -jnp.inf, F32)
    nt = (((1,), (1,)), ((), ()))

    def scores_into(s_ref, j, h):
        k = k_ref[0, pl.ds(pl.multiple_of(j * tq, tq), tq), head_lanes(h)]
        for mi in range(2):
            s_ref[mi] = lax.dot_general(k, qm_ref[h, mi], nt, preferred_element_type=F32)

    def softmax_pv(s_ref, j, h, masked):
        vt = vt_ref[h, j]
        if masked:
            key_chunk = lax.broadcasted_iota(jnp.int32, (tq, tq), 0) // CHUNK
            qry_chunk = lax.broadcasted_iota(jnp.int32, (tq, tq), 1) // CHUNK
            visible = key_chunk <= qry_chunk
            read = lambda mi: jnp.where(visible, s_ref[mi], -jnp.inf)
        else:
            read = lambda mi: s_ref[mi]
        for mi in range(2):
            m_old = m_ref[h, mi, 0:1, :]
            m_new = jnp.maximum(m_old, jnp.max(read(mi), axis=0, keepdims=True))
            alpha = jnp.exp2(m_old - m_new)
            p = jnp.exp2(read(mi) - m_new).astype(BF16)
            pv = jnp.dot(vt, p, preferred_element_type=F32)
            acc_ref[h, mi] = alpha * acc_ref[h, mi] + pv
            m_ref[h, mi] = jnp.broadcast_to(m_new, m_ref.shape[2:])

    scores_into(s_ref.at[0], 0, 0)

    def one_tile(j, masked):
        for h in range(heads):
            if h + 1 < heads:
                scores_into(s_ref.at[h + 1], j, h + 1)
            elif not masked:
                scores_into(s_ref.at[0], j + 1, 0)
            softmax_pv(s_ref.at[h], j, h, masked)

    def two_tiles(t, carry):
        one_tile(2 * t, False)
        one_tile(2 * t + 1, False)
        return carry

    lax.fori_loop(0, lax.shift_right_logical(qi, 1), two_tiles, 0)

    @pl.when(lax.bitwise_and(qi, 1) == 1)
    def _():
        one_tile(qi - 1, False)

    one_tile(qi, True)

    lam = lam_ref[0, 0]
    for h in range(heads):
        o1 = acc_ref[h, 0, 0:dv, :] * (1.0 / acc_ref[h, 0, dv:dv + 1, :])
        o2 = acc_ref[h, 1, 0:dv, :] * (lam / acc_ref[h, 1, dv:dv + 1, :])
        a = o1 - o2
        ms = jnp.mean(a * a, axis=0, keepdims=True)
        y = ((a * lax.rsqrt(ms + EPS)) * g_ref[...]) * out_scale
        o_ref[0, :, head_lanes(h)] = y.T.astype(o_ref.dtype)


def _diff_attention(qkv, lam, subln_g_col, cast_weights=(), *, batch, seq, out_scale,
                    tq=ATTN_TILE, heads=ATTN_HEADS_PER_STEP):
    assert ATTN_HEADS % heads == 0
    groups = ATTN_HEADS // heads
    n_tiles = seq // tq
    acc_rows = HEAD_V_DIM + ATTN_ONES_ROWS
    width = heads * HEAD_V_DIM
    n_steps = batch * groups * n_tiles
    step = lambda b, g, i: (b * groups + g) * n_tiles + i
    cast_in_specs, cast_out_specs, cast_shapes = [], [], []
    for w, col0, ncols in cast_weights:
        assert w.shape[0] % (n_steps * BF16_ROW_TILE) == 0 and col0 % ncols == 0
        rows = w.shape[0] // n_steps
        cast_in_specs.append(pl.BlockSpec(
            (rows, ncols), lambda b, g, i, cb=col0 // ncols: (step(b, g, i), cb)))
        cast_out_specs.append(pl.BlockSpec((rows, ncols), lambda b, g, i: (step(b, g, i), 0)))
        cast_shapes.append(jax.ShapeDtypeStruct((w.shape[0], ncols), BF16))
    return pl.pallas_call(
        functools.partial(_attn_kernel, tq=tq, n_tiles=n_tiles, heads=heads, out_scale=out_scale,
                          n_cast=len(cast_weights)),
        grid=(batch, groups, n_tiles),
        in_specs=[
            pl.BlockSpec(memory_space=pltpu.SMEM),
            pl.BlockSpec((HEAD_V_DIM, 1), lambda b, g, i: (0, 0)),
            pl.BlockSpec((1, tq, width), lambda b, g, i: (b, i, g)),
            pl.BlockSpec((1, seq, width), lambda b, g, i: (b, 0, groups + g)),
            pl.BlockSpec((1, seq, width), lambda b, g, i: (b, 0, 2 * groups + g)),
        ] + cast_in_specs,
        out_specs=[pl.BlockSpec((1, tq, width), lambda b, g, i: (b, i, g))] + cast_out_specs,
        out_shape=[jax.ShapeDtypeStruct((batch, seq, ATTN_HEADS * HEAD_V_DIM), BF16)]
        + cast_shapes,
        scratch_shapes=[pltpu.VMEM((heads, n_tiles, acc_rows, tq), BF16),
                        pltpu.VMEM((heads, 2, tq, HEAD_V_DIM), BF16),
                        pltpu.VMEM((heads, 2, acc_rows, tq), F32),
                        pltpu.VMEM((heads, 2, SUBLANES, tq), F32),
                        pltpu.VMEM((heads, 2, tq, tq), F32)],
        compiler_params=_params("arbitrary", "arbitrary", "arbitrary"),
        name="diff_attention",
    )(lam, subln_g_col, qkv, qkv, qkv, *[w for w, _, _ in cast_weights])


def _causal_conv(x, tail, cw, cb):
    ts = x.shape[0]
    xp = jnp.concatenate([tail, x], axis=0)
    xc = cb
    for j in range(CONV_WIDTH):
        off = SUBLANES - (CONV_WIDTH - 1) + j
        xc = xc + cw[j:j + 1, :] * xp[off:off + ts]
    return xc


def _block_gate_logits(xb, w_ref, b_ref):
    blocks = [jnp.dot(xb[:, n * LRU_BLOCK_DIM:(n + 1) * LRU_BLOCK_DIM], w_ref[n],
                      preferred_element_type=F32) + b_ref[n]
              for n in range(xb.shape[1] // LRU_BLOCK_DIM)]
    return jnp.concatenate(blocks, axis=1)


def _lru_scan(xc, r_logit, i_logit, gate, h0, log_sig):
    ts, c = xc.shape
    r = jax.nn.sigmoid(r_logit)
    i = jax.nn.sigmoid(i_logit)
    log_a = LRU_C * r * log_sig
    a = jnp.exp(log_a)
    b = jnp.sqrt(-jnp.tanh(log_a) * (1.0 + a * a)) * (i * xc)

    groups = ts // SUBLANES
    a = a.reshape(groups, SUBLANES, c)
    b = b.reshape(groups, SUBLANES, c)
    frame = lax.broadcasted_iota(jnp.int32, (1, SUBLANES, c), 1)
    d = 1
    while d < SUBLANES:
        keep = frame >= d
        b = b + a * jnp.where(keep, pltpu.roll(b, d, axis=1), 0.0)
        a = a * jnp.where(keep, pltpu.roll(a, d, axis=1), 1.0)
        d *= 2
    carry = h0
    hs = []
    for gi in range(groups):
        hg = b[gi] + a[gi] * carry
        carry = hg[SUBLANES - 1:SUBLANES, :]
        hs.append(hg)
    return jnp.concatenate(hs, axis=0) * jax.nn.gelu(gate, approximate=True), carry


def _rec_branch_kernel(h_ref, rstd_ref, wx_ref, wg_ref, cw_ref, cb_ref, wr_ref, br_ref, wi_ref,
                       bi_ref, lam_ref, o_ref, xr_ref, gr_ref, xc_ref, rl_ref, il_ref,
                       tail_ref, state_ref, *, nj, tiles_per_seq):
    s = pl.program_id(0)
    prev = jnp.maximum(s - 1, 0)
    ip = prev // nj
    jp = lax.rem(prev, nj)
    ts = xr_ref.shape[0]

    @pl.when(s == 0)
    def _():
        xr_ref[...] = jnp.zeros_like(xr_ref)
        gr_ref[...] = jnp.zeros_like(gr_ref)
        tail_ref[...] = jnp.zeros_like(tail_ref)
        state_ref[...] = jnp.zeros_like(state_ref)

    seq_start = lax.rem(ip, tiles_per_seq) == 0
    tail = jnp.where(seq_start, 0.0, tail_ref[jp])
    carry = jnp.where(seq_start, 0.0, state_ref[jp, 0:1, :])
    cw, cb = cw_ref[...], cb_ref[...]
    slabs = [(r0, r0 + LRU_ROW_SLAB) for r0 in range(0, ts, LRU_ROW_SLAB)]

    for r0, r1 in slabs:
        before = tail if r0 == 0 else xr_ref[r0 - SUBLANES:r0, :]
        xc_ref[r0:r1, :] = _causal_conv(xr_ref[r0:r1, :], before, cw, cb)
    tail_ref[jp] = xr_ref[ts - SUBLANES:ts, :]

    xr_ref[...] = jnp.dot(h_ref[...], wx_ref[...], preferred_element_type=F32) * rstd_ref[...]
    xb = xc_ref[...].astype(BF16)
    rl_ref[...] = _block_gate_logits(xb, wr_ref, br_ref)
    il_ref[...] = _block_gate_logits(xb, wi_ref, bi_ref)

    lam = lam_ref[...]
    log_sig = jnp.minimum(lam, 0.0) - jnp.log1p(jnp.exp(-jnp.abs(lam)))
    for r0, r1 in slabs:
        rec, carry = _lru_scan(xc_ref[r0:r1, :], rl_ref[r0:r1, :], il_ref[r0:r1, :],
                               gr_ref[r0:r1, :], carry, log_sig)
        o_ref[r0:r1, :] = rec.astype(o_ref.dtype)
    state_ref[jp] = jnp.broadcast_to(carry, state_ref.shape[1:])
    gr_ref[...] = jnp.dot(h_ref[...], wg_ref[...], preferred_element_type=F32) * rstd_ref[...]


def _rec_branch(h, rstd, w_x, w_g, conv_w, conv_b, w_r, b_r, w_i, b_i, lam, *, seq,
                tm=LRU_TILE, tc=LRU_CHANNEL_TILE):
    m, k = h.shape
    d_lru = LRU_BLOCKS * LRU_BLOCK_DIM
    assert seq % tm == 0 and d_lru % tc == 0 and w_x.shape == w_g.shape == (k, d_lru)
    nj = d_lru // tc
    n_tiles = (m // tm) * nj
    nblk = tc // LRU_BLOCK_DIM
    this_i = lambda s: jnp.minimum(s, n_tiles - 1) // nj
    this_j = lambda s: lax.rem(jnp.minimum(s, n_tiles - 1), nj)
    prev_i = lambda s: jnp.maximum(s - 1, 0) // nj
    prev_j = lambda s: lax.rem(jnp.maximum(s - 1, 0), nj)
    vec = lambda s: (0, prev_j(s))
    blk = lambda s: (prev_j(s), 0, 0)
    return pl.pallas_call(
        functools.partial(_rec_branch_kernel, nj=nj, tiles_per_seq=seq // tm),
        grid=(n_tiles + 1,),
        in_specs=[
            pl.BlockSpec((tm, k), lambda s: (this_i(s), 0)),
            pl.BlockSpec((tm, 1), lambda s: (this_i(s), 0)),
            pl.BlockSpec((k, tc), lambda s: (0, this_j(s))),
            pl.BlockSpec((k, tc), lambda s: (0, this_j(s))),
            pl.BlockSpec((CONV_WIDTH, tc), vec),
            pl.BlockSpec((1, tc), vec),
            pl.BlockSpec((nblk, LRU_BLOCK_DIM, LRU_BLOCK_DIM), blk),
            pl.BlockSpec((nblk, 1, LRU_BLOCK_DIM), blk),
            pl.BlockSpec((nblk, LRU_BLOCK_DIM, LRU_BLOCK_DIM), blk),
            pl.BlockSpec((nblk, 1, LRU_BLOCK_DIM), blk),
            pl.BlockSpec((1, tc), vec),
        ],
        out_specs=pl.BlockSpec((tm, tc), lambda s: (prev_i(s), prev_j(s))),
        out_shape=jax.ShapeDtypeStruct((m, d_lru), BF16),
        scratch_shapes=[pltpu.VMEM((tm, tc), F32), pltpu.VMEM((tm, tc), F32),
                        pltpu.VMEM((tm, tc), F32),
                        pltpu.VMEM((tm, tc), F32), pltpu.VMEM((tm, tc), F32),
                        pltpu.VMEM((nj, SUBLANES, tc), F32),
                        pltpu.VMEM((nj, SUBLANES, tc), F32)],
        compiler_params=_params("arbitrary"),
        name="rec_branch",
    )(h, rstd, w_x, w_g, conv_w, conv_b, w_r, b_r, w_i, b_i, lam)


def _out_proj_kernel(a_ref, r_ref, wa_ref, wr_ref, x_ref, g_ref, o_ref, xg_ref, rstd_ref, ss_ref,
                     *, n):
    j = pl.program_id(1)
    acc = jnp.dot(a_ref[...], wa_ref[...], preferred_element_type=F32) + x_ref[...]
    x1 = jnp.dot(r_ref[...], wr_ref[...], preferred_element_type=F32) + acc
    o_ref[...] = x1
    xg_ref[...] = (x1 * g_ref[...]).astype(xg_ref.dtype)
    ss = jnp.sum(x1 * x1, axis=-1, keepdims=True)

    @pl.when(j == 0)
    def _():
        ss_ref[...] = ss

    @pl.when(j > 0)
    def _():
        ss_ref[...] += ss

    @pl.when(j == pl.num_programs(1) - 1)
    def _():
        rstd_ref[...] = lax.rsqrt(ss_ref[...] * (1.0 / n) + EPS)


def _out_proj(attn, rec, w_out, x, g_next, *, tm, tn):
    m, ka = attn.shape
    kr = rec.shape[1]
    n = w_out.shape[1]
    assert ka == kr
    tile = lambda i, j: (i, j)
    return pl.pallas_call(
        functools.partial(_out_proj_kernel, n=n),
        grid=(m // tm, n // tn),
        in_specs=[
            pl.BlockSpec((tm, ka), lambda i, j: (i, 0)),
            pl.BlockSpec((tm, kr), lambda i, j: (i, 0)),
            pl.BlockSpec((ka, tn), lambda i, j: (0, j)),
            pl.BlockSpec((kr, tn), lambda i, j: (1, j)),
            pl.BlockSpec((tm, tn), tile),
            pl.BlockSpec((1, tn), lambda i, j: (0, j)),
        ],
        out_specs=[pl.BlockSpec((tm, tn), tile), pl.BlockSpec((tm, tn), tile),
                   pl.BlockSpec((tm, 1), lambda i, j: (i, 0))],
        out_shape=[jax.ShapeDtypeStruct((m, n), F32), jax.ShapeDtypeStruct((m, n), BF16),
                   jax.ShapeDtypeStruct((m, 1), F32)],
        scratch_shapes=[pltpu.VMEM((tm, 1), F32)],
        compiler_params=_params("arbitrary", "arbitrary"),
        name="out_proj",
    )(attn, rec, w_out, w_out, x, g_next)


def _rowscaled_matmul_kernel(xg_ref, rstd_ref, w_ref, o_ref, *, relu2, scaled_tiles, scale):
    acc = jnp.dot(xg_ref[...], w_ref[...], preferred_element_type=F32) * rstd_ref[...]
    if relu2:
        acc = jnp.square(jnp.maximum(acc, 0.0))
    if scaled_tiles:
        acc = acc * jnp.where(pl.program_id(1) < scaled_tiles, scale, 1.0).astype(F32)
    o_ref[...] = acc.astype(o_ref.dtype)


def _rowscaled_matmul(xg, rstd, w, *, out_dtype, tm, tn, name, relu2=False, scaled_cols=0,
                      scale=1.0, fuse_w_producer=False):
    m, k = xg.shape
    n = w.shape[1]
    assert scaled_cols % tn == 0
    return pl.pallas_call(
        functools.partial(_rowscaled_matmul_kernel, relu2=relu2, scaled_tiles=scaled_cols // tn,
                          scale=scale),
        grid=(m // tm, n // tn),
        in_specs=[
            pl.BlockSpec((tm, k), lambda i, j: (i, 0)),
            pl.BlockSpec((tm, 1), lambda i, j: (i, 0)),
            pl.BlockSpec((k, tn), lambda i, j: (0, j)),
        ],
        out_specs=pl.BlockSpec((tm, tn), lambda i, j: (i, j)),
        out_shape=jax.ShapeDtypeStruct((m, n), out_dtype),
        compiler_params=_params(
            "arbitrary", "arbitrary",
            allow_input_fusion=[False, False, True] if fuse_w_producer else None),
        name=name,
    )(xg, rstd, w)


def _norm_prep_kernel(x_ref, g_ref, xg_ref, rstd_ref):
    for r0 in range(0, x_ref.shape[0], NORM_SLAB_ROWS):
        rows = slice(r0, r0 + NORM_SLAB_ROWS)
        x = x_ref[rows, :]
        xg_ref[rows, :] = (x * g_ref[...]).astype(xg_ref.dtype)
        rstd_ref[rows, :] = lax.rsqrt(jnp.mean(x * x, axis=-1, keepdims=True) + EPS)


def _norm_prep(x, g, *, tm):
    m, k = x.shape
    return pl.pallas_call(
        _norm_prep_kernel,
        grid=(m // tm,),
        in_specs=[pl.BlockSpec((tm, k), lambda i: (i, 0)), pl.BlockSpec((1, k), lambda i: (0, 0))],
        out_specs=[pl.BlockSpec((tm, k), lambda i: (i, 0)), pl.BlockSpec((tm, 1), lambda i: (i, 0))],
        out_shape=[jax.ShapeDtypeStruct((m, k), BF16), jax.ShapeDtypeStruct((m, 1), F32)],
        compiler_params=_params("arbitrary"),
        name="norm_prep",
    )(x, g)


def _down_norm_kernel(u_ref, w_ref, x_ref, g_ref, o_ref):
    k = pl.program_id(1)
    pw = x_ref.shape[1]

    @pl.when(k == 0)
    def _():
        o_ref[:, :pw] = x_ref[...]
        o_ref[:, pw:] = jnp.zeros((o_ref.shape[0], o_ref.shape[1] - pw), F32)

    o_ref[...] = jnp.dot(u_ref[...], w_ref[...], preferred_element_type=F32) + o_ref[...]

    for p in range(1, o_ref.shape[1] // pw):
        @pl.when(k == p)
        def _(p=p):
            o_ref[:, p * pw:(p + 1) * pw] += x_ref[...]

    @pl.when(k == pl.num_programs(1) - 1)
    def _():
        for r0 in range(0, o_ref.shape[0], NORM_SLAB_ROWS):
            rows = slice(r0, r0 + NORM_SLAB_ROWS)
            o_ref[rows, :] = _rmsnorm_rows(o_ref[rows, :], g_ref[...])


def _down_norm(u, w, x, g, *, tm, tk):
    m, kk = u.shape
    n = w.shape[1]
    pieces = DOWN_RESIDUAL_PIECES
    assert kk // tk >= pieces and n % pieces == 0
    return pl.pallas_call(
        _down_norm_kernel,
        grid=(m // tm, kk // tk),
        in_specs=[
            pl.BlockSpec((tm, tk), lambda i, k: (i, k)),
            pl.BlockSpec((tk, n), lambda i, k: (k, 0)),
            pl.BlockSpec((tm, n // pieces), lambda i, k: (i, jnp.minimum(k, pieces - 1))),
            pl.BlockSpec((1, n), lambda i, k: (0, 0)),
        ],
        out_specs=pl.BlockSpec((tm, n), lambda i, k: (i, 0)),
        out_shape=jax.ShapeDtypeStruct((m, n), F32),
        compiler_params=_params("arbitrary", "arbitrary"),
        name="down_norm",
    )(u, w, x, g)


def kernel(x, norm_mix_g, w_in, conv_w, conv_b, w_rgate, b_rgate, w_igate, b_igate, lru_lambda, lambda_q1, lambda_k1, lambda_q2, lambda_k2, subln_g, w_out, norm_mlp_g, w_mlp_up, w_mlp_down, norm_final_g):
    batch, seq, d_model = x.shape
    depth = w_in.shape[0]
    d_attn = ATTN_HEADS * HEAD_V_DIM
    d_lru = LRU_BLOCKS * LRU_BLOCK_DIM
    m = batch * seq
    row = lambda v: v.reshape(1, -1)

    xf = x.reshape(m, d_model)
    for l in range(depth):
        lambda_init = 0.8 - 0.6 * math.exp(-0.3 * l)
        xg_mix, rstd_mix = _norm_prep(xf, row(norm_mix_g[l]), tm=NORM_PREP_ROWS)
        w_qkv = w_in[l][:, :3 * d_attn].astype(BF16)
        qkv = _rowscaled_matmul(xg_mix, rstd_mix, w_qkv, out_dtype=BF16, scaled_cols=d_attn,
                                scale=ATTN_Q_PRESCALE, name="qkv_proj", fuse_w_producer=True,
                                **QKV_PROJ_TILES)

        lam = _diff_lambda(row(lambda_q1[l]), row(lambda_k1[l]), row(lambda_q2[l]),
                           row(lambda_k2[l]), lambda_init)
        full = lambda w: (w, 0, w.shape[1])
        attn, w_x_l, w_g_l, w_out_l, w_up_l, w_down_l = _diff_attention(
            qkv.reshape(batch, seq, 3 * d_attn), lam, subln_g[l].reshape(HEAD_V_DIM, 1),
            ((w_in[l], 3 * d_attn, d_lru), (w_in[l], 3 * d_attn + d_lru, d_lru),
             full(w_out[l]), full(w_mlp_up[l]), full(w_mlp_down[l])),
            batch=batch, seq=seq, out_scale=1.0 - lambda_init)

        rec = _rec_branch(
            xg_mix, rstd_mix, w_x_l, w_g_l, conv_w[l], row(conv_b[l]),
            w_rgate[l].astype(BF16), b_rgate[l].reshape(LRU_BLOCKS, 1, LRU_BLOCK_DIM),
            w_igate[l].astype(BF16), b_igate[l].reshape(LRU_BLOCKS, 1, LRU_BLOCK_DIM),
            row(lru_lambda[l]), seq=seq)

        xf, xg, rstd = _out_proj(attn.reshape(m, d_attn), rec, w_out_l, xf, row(norm_mlp_g[l]),
                                 **OUT_PROJ_TILES)

        u = _rowscaled_matmul(xg, rstd, w_up_l, out_dtype=BF16, relu2=True, name="up_proj",
                              **UP_PROJ_TILES)
        last = l == depth - 1
        assert last, "DEPTH > 1 needs an un-normalised down-projection variant"
        xf = _down_norm(u, w_down_l, xf, row(norm_final_g), **DOWN_PROJ_TILES)
    return xf.reshape(batch, seq, d_model)
```
